```python
import jax
import jax.numpy as jnp
from jax import lax
import numpy as np

D_MODEL = 1024
BATCH = 4
SEQ = 8192
DEPTH = 2

CTX_LEN = 256
GRID_W = 64

POOL_WINDOWS = (2, 4, 8, 16)
POOL_WIDTH = 256
POOL_GROUP = POOL_WIDTH // len(POOL_WINDOWS)
SGU_WIDTH = 256
SGU_GROUPS = 4
SGU_GROUP = SGU_WIDTH // SGU_GROUPS
CHUNK = 128
MLA_HEADS = 8
QK_NOPE = 64
QK_ROPE = 32
V_HEAD = 64
Q_LORA = 384
KV_LORA = 256
MLA_WIDTH = MLA_HEADS * V_HEAD
QK_HEAD = QK_NOPE + QK_ROPE
SM_SCALE = QK_HEAD ** -0.5
Q_BLOCK = 128
ROPE_BASE = 10000.0
ROPE_AXIS = QK_ROPE // 2
ROPE_PAIRS = ROPE_AXIS // 2
N_BRANCH = 3
OFF_U = POOL_WIDTH
OFF_V = OFF_U + SGU_WIDTH
OFF_QA = OFF_V + SGU_WIDTH
OFF_KVA = OFF_QA + Q_LORA
OFF_KR = OFF_KVA + KV_LORA
OFF_GATE = OFF_KR + QK_ROPE
IN_WIDTH = OFF_GATE + N_BRANCH * D_MODEL
N_EXPERTS = 16
N_GROUPS = 4
EXPERTS_PER_GROUP = N_EXPERTS // N_GROUPS
TOP_K = 2
D_EXPERT = 512
EPS = 1e-6

kernel_name = "hybrid_pool_sgu_mla_moe_diffusion_trunk"


def rmsnorm(x, g):
    xf = x.astype(jnp.float32)
    y = xf * lax.rsqrt(jnp.mean(xf * xf, axis=-1, keepdims=True) + EPS)
    return (y * g.astype(jnp.float32)).astype(x.dtype)


def adaln(cond, w_mod, b_mod):
    m = jax.nn.silu(cond) @ w_mod + b_mod
    return [a[:, None, :] for a in jnp.split(m, 6, axis=-1)]


def modulate(h, shift, scale):
    return h * (1 + scale) + shift


def axial_rope_tables(n):
    rows = n // GRID_W
    row = jnp.repeat(jnp.arange(rows, dtype=jnp.float32), GRID_W)
    col = jnp.tile(jnp.arange(GRID_W, dtype=jnp.float32), rows)
    inv = ROPE_BASE ** (-(jnp.arange(ROPE_PAIRS, dtype=jnp.float32) * 2.0 / ROPE_AXIS))
    ang_r = row[:, None] * inv
    ang_c = col[:, None] * inv
    ang = jnp.concatenate([ang_r, ang_r, ang_c, ang_c], axis=-1)
    return jnp.cos(ang), jnp.sin(ang)


def apply_rope(x, cos, sin):
    x4 = x.reshape(x.shape[:-1] + (4, ROPE_PAIRS))
    rot = jnp.stack([-x4[..., 1, :], x4[..., 0, :], -x4[..., 3, :], x4[..., 2, :]], axis=-2).reshape(x.shape)
    return (x.astype(jnp.float32) * cos + rot.astype(jnp.float32) * sin).astype(x.dtype)


def split_proj(proj):
    return jnp.split(proj, [OFF_U, OFF_V, OFF_QA, OFF_KVA, OFF_KR, OFF_GATE], axis=-1)


def pool_mixer(p, pool_w, pool_scale):
    bsz, n, _ = p.shape
    pf = p.astype(jnp.float32)
    cs = jnp.concatenate([jnp.zeros_like(pf[:, :1]), jnp.cumsum(pf, axis=1)], axis=1)
    t = jnp.arange(n)
    outs = []
    for gi, w in enumerate(POOL_WINDOWS):
        lo = jnp.clip(t - w // 2, 0, n)
        hi = jnp.clip(t + (w - w // 2), 0, n)
        csg = cs[..., gi * POOL_GROUP:(gi + 1) * POOL_GROUP]
        cnt = (hi - lo).astype(jnp.float32)[None, :, None]
        mean = (jnp.take(csg, hi, axis=1) - jnp.take(csg, lo, axis=1)) / cnt
        outs.append(mean - pf[..., gi * POOL_GROUP:(gi + 1) * POOL_GROUP])
    d = jnp.stack(outs, axis=2).astype(p.dtype)
    y = jnp.einsum('bngc,gcd->bngd', d, pool_w).reshape(bsz, n, POOL_WIDTH)
    return y * pool_scale


def sgu_mixer(u, v, norm_g, ws, b):
    bsz, n, _ = v.shape
    u = jax.nn.gelu(u)
    v = rmsnorm(jax.nn.gelu(v), norm_g)
    vc = v.reshape(bsz, n // CHUNK, CHUNK, SGU_GROUPS, SGU_GROUP)
    mixed = jnp.einsum('gij,bnjgc->bnigc', ws, vc) + b.T[:, :, None]
    return u * mixed.reshape(bsz, n, SGU_WIDTH)


def mla_q(q_a, lp, rope_tabs):
    bsz, n, _ = q_a.shape
    q = (rmsnorm(q_a, lp['qa_norm_g']) @ lp['w_uq']).reshape(bsz, n, MLA_HEADS, QK_HEAD)
    q_nope = rmsnorm(q[..., :QK_NOPE], lp['q_norm_g'][:QK_NOPE])
    q_rope = rmsnorm(q[..., QK_NOPE:], lp['q_norm_g'][QK_NOPE:])
    if rope_tabs is not None:
        cos, sin = rope_tabs
        q_rope = apply_rope(q_rope, cos[:, None, :], sin[:, None, :])
    return q_nope, q_rope


def mla_kv(kv_a, k_r, lp, rope_tabs):
    bsz, n, _ = kv_a.shape
    kv = (rmsnorm(kv_a, lp['kva_norm_g']) @ lp['w_ukv']).reshape(bsz, n, MLA_HEADS, QK_NOPE + V_HEAD)
    k_nope = rmsnorm(kv[..., :QK_NOPE], lp['k_norm_g'][:QK_NOPE])
    v = kv[..., QK_NOPE:]
    k_rope = rmsnorm(k_r, lp['k_norm_g'][QK_NOPE:])
    if rope_tabs is not None:
        cos, sin = rope_tabs
        k_rope = apply_rope(k_rope, cos, sin)
    return (k_nope, k_rope, v)


def attend(q_nope, q_rope, k_nope, k_rope, v):
    bsz, n, heads, _ = q_nope.shape
    nb = n // Q_BLOCK

    def blocks(a):
        return a.reshape((bsz, nb, Q_BLOCK) + a.shape[2:]).swapaxes(0, 1)

    def one(args):
        qn, qr = args
        s = jnp.einsum('bqhd,bkhd->bhqk', qn, k_nope) + jnp.einsum('bqhr,bkr->bhqk', qr, k_rope)
        p = jax.nn.softmax(s.astype(jnp.float32) * SM_SCALE, axis=-1).astype(v.dtype)
        return jnp.einsum('bhqk,bkhd->bqhd', p, v)

    o = lax.map(one, (blocks(q_nope), blocks(q_rope)))
    return o.swapaxes(0, 1).reshape(bsz, n, heads * V_HEAD)


def hybrid_mixer(p, u, v, q_a, g, keys, lp, rope_tabs):
    y_pool = pool_mixer(p, lp['pool_w'], lp['pool_scale'])
    y_sgu = sgu_mixer(u, v, lp['sgu_norm_g'], lp['sgu_ws'], lp['sgu_b'])
    q_nope, q_rope = mla_q(q_a, lp, rope_tabs)
    y_att = attend(q_nope, q_rope, *keys)
    gp, gs, ga = jnp.split(jax.nn.sigmoid(g + lp['b_gate']), N_BRANCH, axis=-1)
    merged = (gp * (y_pool @ lp['w_br_pool'])
              + gs * (y_sgu @ lp['w_br_sgu'])
              + ga * (y_att @ lp['w_br_mla']))
    return merged @ lp['w_out']


def moe(h, router_w, router_b, w_gate, w_up, w_down):
    shp = h.shape
    t = h.reshape(-1, shp[-1])
    scores = jax.nn.sigmoid((t @ router_w).astype(jnp.float32))
    sel = scores + router_b.astype(jnp.float32)
    grp_score = lax.top_k(sel.reshape(-1, N_GROUPS, EXPERTS_PER_GROUP), 2)[0].sum(-1)
    best = jnp.argmax(grp_score, axis=-1)
    in_grp = (jnp.arange(N_EXPERTS) // EXPERTS_PER_GROUP)[None, :] == best[:, None]
    _, idx = lax.top_k(jnp.where(in_grp, sel, -jnp.inf), TOP_K)
    w = jnp.take_along_axis(scores, idx, axis=-1)
    w = w / jnp.sum(w, axis=-1, keepdims=True)
    gate = jnp.sum(jax.nn.one_hot(idx, N_EXPERTS, dtype=jnp.float32) * w[..., None], axis=-2).astype(t.dtype)
    y = jnp.zeros_like(t)
    for e in range(N_EXPERTS):
        a = jax.nn.silu(t @ w_gate[e]) * (t @ w_up[e])
        y = y + gate[:, e:e + 1] * (a @ w_down[e])
    return y.reshape(shp)


def setup_inputs(seed: int = 0) -> dict:
    key = jax.random.key(seed)
    ks = jax.random.split(key, 32)
    f32 = jnp.float32
    L, D = DEPTH, D_MODEL

    def nrm(k, shape, scale):
        return jax.random.normal(k, shape, f32) * scale

    def gain(k, shape):
        return 1.0 + 0.05 * jax.random.normal(k, shape, f32)

    return {
        "x": nrm(ks[0], (BATCH, SEQ, D), 1.0),
        "c": nrm(ks[1], (BATCH, D), 1.0),
        "ctx": nrm(ks[2], (BATCH, CTX_LEN, D), 1.0),
        "c_ctx": nrm(ks[3], (D,), 1.0),
        "w_mod": nrm(ks[4], (L, D, 6 * D), 0.5 * D ** -0.5),
        "b_mod": nrm(ks[5], (L, 6 * D), 0.02),
        "norm1_g": gain(ks[6], (L, D)),
        "norm2_g": gain(ks[7], (L, D)),
        "w_in": nrm(ks[8], (L, D, IN_WIDTH), D ** -0.5),
        "pool_w": nrm(ks[9], (L, len(POOL_WINDOWS), POOL_GROUP, POOL_GROUP), POOL_GROUP ** -0.5),
        "pool_scale": gain(ks[10], (L, POOL_WIDTH)),
        "sgu_norm_g": gain(ks[11], (L, SGU_WIDTH)),
        "sgu_ws": nrm(ks[12], (L, SGU_GROUPS, CHUNK, CHUNK), CHUNK ** -0.5),
        "sgu_b": gain(ks[13], (L, SGU_GROUPS, CHUNK)),
        "qa_norm_g": gain(ks[14], (L, Q_LORA)),
        "w_uq": nrm(ks[15], (L, Q_LORA, MLA_HEADS * QK_HEAD), Q_LORA ** -0.5),
        "kva_norm_g": gain(ks[16], (L, KV_LORA)),
        "w_ukv": nrm(ks[17], (L, KV_LORA, MLA_HEADS * (QK_NOPE + V_HEAD)), KV_LORA ** -0.5),
        "q_norm_g": gain(ks[18], (L, QK_HEAD)),
        "k_norm_g": gain(ks[19], (L, QK_HEAD)),
        "w_br_pool": nrm(ks[20], (L, POOL_WIDTH, D), POOL_WIDTH ** -0.5),
        "w_br_sgu": nrm(ks[21], (L, SGU_WIDTH, D), SGU_WIDTH ** -0.5),
        "w_br_mla": nrm(ks[22], (L, MLA_WIDTH, D), MLA_WIDTH ** -0.5),
        "b_gate": nrm(ks[23], (L, N_BRANCH * D), 0.02),
        "w_out": nrm(ks[24], (L, D, D), D ** -0.5),
        "router_w": nrm(ks[25], (D, N_EXPERTS), D ** -0.5),
        "router_b": nrm(ks[26], (N_EXPERTS,), 0.01),
        "w_e_gate": nrm(ks[27], (L, N_EXPERTS, D, D_EXPERT), D ** -0.5),
        "w_e_up": nrm(ks[28], (L, N_EXPERTS, D, D_EXPERT), D ** -0.5),
        "w_e_down": nrm(ks[29], (L, N_EXPERTS, D_EXPERT, D), D_EXPERT ** -0.5),
    }


def reference(x, c, ctx, c_ctx, w_mod, b_mod, norm1_g, norm2_g, w_in, pool_w, pool_scale,
              sgu_norm_g, sgu_ws, sgu_b, qa_norm_g, w_uq, kva_norm_g, w_ukv, q_norm_g, k_norm_g,
              w_br_pool, w_br_sgu, w_br_mla, b_gate, w_out, router_w, router_b,
              w_e_gate, w_e_up, w_e_down):
    n_lat = x.shape[1]
    rope_tabs = axial_rope_tables(n_lat)
    x_lat, x_ctx = x, ctx
    for l in range(DEPTH):
        last = l == DEPTH - 1
        lp = dict(w_in=w_in[l], pool_w=pool_w[l], pool_scale=pool_scale[l], sgu_norm_g=sgu_norm_g[l],
                  sgu_ws=sgu_ws[l], sgu_b=sgu_b[l], qa_norm_g=qa_norm_g[l], w_uq=w_uq[l],
                  kva_norm_g=kva_norm_g[l], w_ukv=w_ukv[l], q_norm_g=q_norm_g[l], k_norm_g=k_norm_g[l],
                  w_br_pool=w_br_pool[l], w_br_sgu=w_br_sgu[l], w_br_mla=w_br_mla[l],
                  b_gate=b_gate[l], w_out=w_out[l])
        sh1, sc1, g1, sh2, sc2, g2 = adaln(c, w_mod[l], b_mod[l])
        csh1, csc1, cg1, csh2, csc2, cg2 = adaln(c_ctx[None, :], w_mod[l], b_mod[l])

        hc = modulate(rmsnorm(x_ctx, norm1_g[l]), csh1, csc1)
        if last:
            kv_cols = hc @ lp['w_in'][:, OFF_KVA:OFF_GATE]
            kv_a_c, k_r_c = kv_cols[..., :KV_LORA], kv_cols[..., KV_LORA:]
        else:
            pc, uc, vc, qac, kv_a_c, k_r_c, gc = split_proj(hc @ lp['w_in'])
        ctx_keys = mla_kv(kv_a_c, k_r_c, lp, None)

        h = modulate(rmsnorm(x_lat, norm1_g[l]), sh1, sc1)
        p, u, v, q_a, kv_a, k_r, g = split_proj(h @ lp['w_in'])
        lat_keys = mla_kv(kv_a, k_r, lp, rope_tabs)
        keys = tuple(jnp.concatenate([a, b], axis=1) for a, b in zip(lat_keys, ctx_keys))
        x_lat = x_lat + g1 * hybrid_mixer(p, u, v, q_a, g, keys, lp, rope_tabs)
        h2 = modulate(rmsnorm(x_lat, norm2_g[l]), sh2, sc2)
        x_lat = x_lat + g2 * moe(h2, router_w, router_b, w_e_gate[l], w_e_up[l], w_e_down[l])

        if not last:
            x_ctx = x_ctx + cg1 * hybrid_mixer(pc, uc, vc, qac, gc, ctx_keys, lp, None)
            hc2 = modulate(rmsnorm(x_ctx, norm2_g[l]), csh2, csc2)
            x_ctx = x_ctx + cg2 * moe(hc2, router_w, router_b, w_e_gate[l], w_e_up[l], w_e_down[l])
    return x_lat
```

```python
import functools
import math

import jax
import jax.numpy as jnp
from jax import lax
from jax.experimental import pallas as pl
from jax.experimental.pallas import tpu as pltpu

F32 = jnp.float32
BF16 = jnp.bfloat16

GRID_W = 64
POOL_WINDOWS = (2, 4, 8, 16)
POOL_WIDTH = 256
POOL_GROUP = POOL_WIDTH // len(POOL_WINDOWS)
SGU_WIDTH = 256
SGU_GROUPS = 4
SGU_GROUP = SGU_WIDTH // SGU_GROUPS
CHUNK = 128
HEADS = 8
QK_NOPE = 64
QK_ROPE = 32
V_HEAD = 64
Q_LORA = 384
KV_LORA = 256
QK_HEAD = QK_NOPE + QK_ROPE
MLA_WIDTH = HEADS * V_HEAD
SM_SCALE = QK_HEAD ** -0.5
ROPE_BASE = 10000.0
ROPE_AXIS = QK_ROPE // 2
ROPE_PAIRS = ROPE_AXIS // 2
N_BRANCH = 3
OFF_U = POOL_WIDTH
OFF_V = OFF_U + SGU_WIDTH
OFF_QA = OFF_V + SGU_WIDTH
OFF_KVA = OFF_QA + Q_LORA
OFF_KR = OFF_KVA + KV_LORA
OFF_GATE = OFF_KR + QK_ROPE
N_EXPERTS = 16
N_GROUPS = 4
EXPERTS_PER_GROUP = N_EXPERTS // N_GROUPS
PAIRS_PER_GROUP = EXPERTS_PER_GROUP * (EXPERTS_PER_GROUP - 1) // 2
N_CLASSES = N_GROUPS * PAIRS_PER_GROUP
EPS = 1e-6

PUV_WIDTH = OFF_QA
QKT_ROWS = OFF_GATE - OFF_QA
HEAD_PAD = 128
QK_EXP2_SCALE = SM_SCALE * math.log2(math.e)

V7X_LANES = 128
V7X_SUBLANES = 8
V7X_VMEM_LIMIT = 52 * 1024 * 1024
KEY_TILE = 256
MOE_TILE = 256


def _cparams(sem, vmem=V7X_VMEM_LIMIT):
    return pltpu.CompilerParams(dimension_semantics=sem, vmem_limit_bytes=vmem)


def _norm_mod(x, g, shift, scale):
    ms = jnp.mean(x * x, axis=-1, keepdims=True)
    y = x * lax.rsqrt(ms + EPS) * g
    return y * (1.0 + scale) + shift


def _rms_rows(x, g):
    ms = jnp.mean(x * x, axis=0, keepdims=True)
    return x * lax.rsqrt(ms + EPS) * g


def _adaln_kernel(c_ref, w_ref, b_ref, o_ref):
    c = c_ref[...]
    s = (c * jax.nn.sigmoid(c)).astype(BF16)
    o_ref[0] = jnp.dot(s, w_ref[0].astype(BF16), preferred_element_type=F32) + b_ref[0]


def _adaln(cond8, w_mod, b_mod):
    depth, d, n6 = w_mod.shape
    tn = 1536
    return pl.pallas_call(
        _adaln_kernel,
        out_shape=jax.ShapeDtypeStruct((depth, V7X_SUBLANES, n6), F32),
        grid=(depth, n6 // tn),
        in_specs=[
            pl.BlockSpec((V7X_SUBLANES, d), lambda l, j: (0, 0)),
            pl.BlockSpec((1, d, tn), lambda l, j: (l, 0, j)),
            pl.BlockSpec((1, 1, tn), lambda l, j: (l, 0, j)),
        ],
        out_specs=pl.BlockSpec((1, V7X_SUBLANES, tn), lambda l, j: (l, 0, j)),
        compiler_params=_cparams(("parallel", "parallel")),
        name="adaln",
    )(cond8, w_mod, b_mod.reshape(depth, 1, n6))


def _inproj_kernel(x_ref, g_ref, sh_ref, sc_ref, wn_ref, wt_ref, puv_ref, qkt_ref):
    h = _norm_mod(x_ref[0], g_ref[...], sh_ref[0], sc_ref[0]).astype(BF16)
    puv_ref[0] = jnp.dot(h, wn_ref[...], preferred_element_type=F32)
    qkt_ref[0] = lax.dot_general(wt_ref[...], h, (((1,), (1,)), ((), ())),
                                 preferred_element_type=F32)


def _inproj(x, g, shift, scale, w_nat, w_t, tm):
    b, n, d = x.shape
    return pl.pallas_call(
        _inproj_kernel,
        out_shape=(jax.ShapeDtypeStruct((b, n, PUV_WIDTH), F32),
                   jax.ShapeDtypeStruct((b, QKT_ROWS, n), F32)),
        grid=(b, n // tm),
        in_specs=[
            pl.BlockSpec((1, tm, d), lambda bi, i: (bi, i, 0)),
            pl.BlockSpec((1, d), lambda bi, i: (0, 0)),
            pl.BlockSpec((1, 1, d), lambda bi, i: (bi, 0, 0)),
            pl.BlockSpec((1, 1, d), lambda bi, i: (bi, 0, 0)),
            pl.BlockSpec((d, PUV_WIDTH), lambda bi, i: (0, 0)),
            pl.BlockSpec((QKT_ROWS, d), lambda bi, i: (0, 0)),
        ],
        out_specs=(pl.BlockSpec((1, tm, PUV_WIDTH), lambda bi, i: (bi, i, 0)),
                   pl.BlockSpec((1, QKT_ROWS, tm), lambda bi, i: (bi, 0, i))),
        compiler_params=_cparams(("parallel", "parallel")),
        name="inproj",
    )(x, g, shift, scale, w_nat, w_t)


POOL_HALO = 8


def _poolsgu_kernel(puv_ref, prev_ref, next_ref, poolw_ref, pscale_ref, sgug_ref, ws_ref,
                    bias_ref, o_ref, *, n, tc):
    i = pl.program_id(1)
    last = pl.num_programs(1) - 1
    puv = puv_ref[0]
    p = puv[:, :POOL_WIDTH]
    prev = jnp.where(i > 0, prev_ref[0], 0.0)
    nxt = jnp.where(i < last, next_ref[0], 0.0)
    e = jnp.concatenate([prev, p, nxt], axis=0)
    rows = tc + 2 * POOL_HALO
    a2 = e + pltpu.roll(e, 1, 0)
    a4 = a2 + pltpu.roll(a2, 2, 0)
    a8 = a4 + pltpu.roll(a4, 4, 0)
    a16 = a8 + pltpu.roll(a8, 8, 0)
    w2 = a2[POOL_HALO:POOL_HALO + tc]
    w4 = pltpu.roll(a4, rows - 1, 0)[POOL_HALO:POOL_HALO + tc]
    w8 = pltpu.roll(a8, rows - 3, 0)[POOL_HALO:POOL_HALO + tc]
    w16 = pltpu.roll(a16, rows - 7, 0)[POOL_HALO:POOL_HALO + tc]
    lane = lax.broadcasted_iota(jnp.int32, (tc, POOL_WIDTH), 1)
    t = i * tc + lax.broadcasted_iota(jnp.int32, (tc, POOL_WIDTH), 0)
    grp = lane // POOL_GROUP
    half = jnp.where(grp == 0, 1, jnp.where(grp == 1, 2, jnp.where(grp == 2, 4, 8)))
    cnt = (jnp.minimum(t + half, n) - jnp.maximum(t - half, 0)).astype(F32)
    wsum = jnp.where(grp == 0, w2, jnp.where(grp == 1, w4, jnp.where(grp == 2, w8, w16)))
    dlt = (wsum / cnt - p).astype(BF16)
    y_pool = jnp.dot(dlt, poolw_ref[...], preferred_element_type=F32) * pscale_ref[...]
    o_ref[0, :, 0:POOL_WIDTH] = y_pool.astype(o_ref.dtype)

    u = jax.nn.gelu(puv[:, OFF_U:OFF_V])
    gv = jax.nn.gelu(puv[:, OFF_V:OFF_QA])
    vn = gv * lax.rsqrt(jnp.mean(gv * gv, axis=-1, keepdims=True) + EPS) * sgug_ref[...]
    clane = lax.broadcasted_iota(jnp.int32, (CHUNK, SGU_WIDTH), 1) // SGU_GROUP
    for c in range(tc // CHUNK):
        vc = vn[c * CHUNK:(c + 1) * CHUNK]
        mixed = bias_ref[...]
        for gi in range(SGU_GROUPS):
            vm = jnp.where(clane == gi, vc, 0.0).astype(BF16)
            mixed = mixed + jnp.dot(ws_ref[gi], vm, preferred_element_type=F32)
        y = u[c * CHUNK:(c + 1) * CHUNK] * mixed
        o_ref[0, c * CHUNK:(c + 1) * CHUNK, POOL_WIDTH:POOL_WIDTH + SGU_WIDTH] = y.astype(o_ref.dtype)


def _poolsgu(puv, poolw, pscale, sgug, ws, bias, tc):
    b, n, _ = puv.shape
    nh = n // POOL_HALO
    per = tc // POOL_HALO
    kern = functools.partial(_poolsgu_kernel, n=n, tc=tc)
    return pl.pallas_call(
        kern,
        out_shape=jax.ShapeDtypeStruct((b, n, POOL_WIDTH + SGU_WIDTH), BF16),
        grid=(b, n // tc),
        in_specs=[
            pl.BlockSpec((1, tc, PUV_WIDTH), lambda bi, i: (bi, i, 0)),
            pl.BlockSpec((1, POOL_HALO, POOL_WIDTH),
                         lambda bi, i: (bi, jnp.maximum(i * per - 1, 0), 0)),
            pl.BlockSpec((1, POOL_HALO, POOL_WIDTH),
                         lambda bi, i: (bi, jnp.minimum((i + 1) * per, nh - 1), 0)),
            pl.BlockSpec((POOL_WIDTH, POOL_WIDTH), lambda bi, i: (0, 0)),
            pl.BlockSpec((1, POOL_WIDTH), lambda bi, i: (0, 0)),
            pl.BlockSpec((1, SGU_WIDTH), lambda bi, i: (0, 0)),
            pl.BlockSpec((SGU_GROUPS, CHUNK, CHUNK), lambda bi, i: (0, 0, 0)),
            pl.BlockSpec((CHUNK, SGU_WIDTH), lambda bi, i: (0, 0)),
        ],
        out_specs=pl.BlockSpec((1, tc, POOL_WIDTH + SGU_WIDTH), lambda bi, i: (bi, i, 0)),
        compiler_params=_cparams(("parallel", "parallel")),
        name="poolsgu",
    )(puv, puv, puv, poolw, pscale, sgug, ws, bias)


def _qkv_kernel(qkt_ref, gqa_ref, gkva_ref, gq_ref, gk_ref, cos_ref, sin_ref, wq_ref, wkv_ref,
                qt_ref, k_ref, vt_ref, *, use_rope, tm):
    blk = qkt_ref[0]
    qa_n = _rms_rows(blk[0:Q_LORA], gqa_ref[...]).astype(BF16)
    kva_n = _rms_rows(blk[Q_LORA:Q_LORA + KV_LORA], gkva_ref[...]).astype(BF16)
    q = jnp.dot(wq_ref[...], qa_n, preferred_element_type=F32)
    kv = jnp.dot(wkv_ref[...], kva_n, preferred_element_type=F32)
    gq = gq_ref[...]
    gk = gk_ref[...]

    def rope(x):
        if not use_rope:
            return x
        r = ROPE_PAIRS
        rot = jnp.concatenate([-x[r:2 * r], x[0:r], -x[3 * r:4 * r], x[2 * r:3 * r]], axis=0)
        return x * cos_ref[...] + rot * sin_ref[...]

    k_rope = rope(_rms_rows(blk[Q_LORA + KV_LORA:QKT_ROWS], gk[QK_NOPE:QK_HEAD]))
    pad = jnp.zeros((HEAD_PAD - QK_HEAD, tm), F32)
    kv_rows = QK_NOPE + V_HEAD
    for h in range(HEADS):
        qh = q[h * QK_HEAD:(h + 1) * QK_HEAD]
        q_nope = _rms_rows(qh[0:QK_NOPE], gq[0:QK_NOPE])
        q_rope = rope(_rms_rows(qh[QK_NOPE:QK_HEAD], gq[QK_NOPE:QK_HEAD]))
        qcat = jnp.concatenate([q_nope, q_rope, pad], axis=0) * QK_EXP2_SCALE
        qt_ref[0, h] = qcat.astype(qt_ref.dtype)
        k_nope = _rms_rows(kv[h * kv_rows:h * kv_rows + QK_NOPE], gk[0:QK_NOPE])
        kcat = jnp.concatenate([k_nope, k_rope, pad], axis=0)
        kt = kcat.T.astype(k_ref.dtype)
        v = kv[h * kv_rows + QK_NOPE:(h + 1) * kv_rows].astype(vt_ref.dtype)
        for j in range(tm // KEY_TILE):
            k_ref[0, h, j] = kt[j * KEY_TILE:(j + 1) * KEY_TILE]
            vt_ref[0, h, j] = v[:, j * KEY_TILE:(j + 1) * KEY_TILE]


def _qkv(qkt, gqa, gkva, gq, gk, cos_t, sin_t, wq_t, wkv_t, use_rope, tm):
    b, _, n = qkt.shape
    nt = n // KEY_TILE
    per = tm // KEY_TILE
    kern = functools.partial(_qkv_kernel, use_rope=use_rope, tm=tm)
    full = lambda shape: pl.BlockSpec(shape, lambda bi, i: (0,) * len(shape))
    return pl.pallas_call(
        kern,
        out_shape=(jax.ShapeDtypeStruct((b, HEADS, HEAD_PAD, n), BF16),
                   jax.ShapeDtypeStruct((b, HEADS, nt, KEY_TILE, HEAD_PAD), BF16),
                   jax.ShapeDtypeStruct((b, HEADS, nt, V_HEAD, KEY_TILE), BF16)),
        grid=(b, n // tm),
        in_specs=[
            pl.BlockSpec((1, QKT_ROWS, tm), lambda bi, i: (bi, 0, i)),
            full((Q_LORA, tm)), full((KV_LORA, tm)), full((QK_HEAD, tm)), full((QK_HEAD, tm)),
            pl.BlockSpec((QK_ROPE, tm), lambda bi, i: (0, i)),
            pl.BlockSpec((QK_ROPE, tm), lambda bi, i: (0, i)),
            full((HEADS * QK_HEAD, Q_LORA)), full((HEADS * (QK_NOPE + V_HEAD), KV_LORA)),
        ],
        out_specs=(pl.BlockSpec((1, HEADS, HEAD_PAD, tm), lambda bi, i: (bi, 0, 0, i)),
                   pl.BlockSpec((1, HEADS, per, KEY_TILE, HEAD_PAD), lambda bi, i: (bi, 0, i, 0, 0)),
                   pl.BlockSpec((1, HEADS, per, V_HEAD, KEY_TILE), lambda bi, i: (bi, 0, i, 0, 0))),
        compiler_params=_cparams(("parallel", "parallel")),
        name="qkv",
    )(qkt, gqa, gkva, gq, gk, cos_t, sin_t, wq_t, wkv_t)


def _attn_kernel(qt_ref, *refs, n_src):
    o_ref = refs[2 * n_src]
    qt = qt_ref[0, 0]
    tq = qt.shape[1]
    carry = (jnp.full((1, tq), -jnp.inf, F32), jnp.zeros((1, tq), F32),
             jnp.zeros((V_HEAD, tq), F32))
    for s in range(n_src):
        k_ref, vt_ref = refs[2 * s], refs[2 * s + 1]

        def body(j, c, k_ref=k_ref, vt_ref=vt_ref):
            m, l, acc = c
            sc = jnp.dot(k_ref[0, 0, j], qt, preferred_element_type=F32)
            m_new = jnp.maximum(m, jnp.max(sc, axis=0, keepdims=True))
            alpha = jnp.exp2(m - m_new)
            p = jnp.exp2(sc - m_new)
            l = alpha * l + jnp.sum(p, axis=0, keepdims=True)
            acc = alpha * acc + jnp.dot(vt_ref[0, 0, j], p.astype(BF16),
                                        preferred_element_type=F32)
            return m_new, l, acc

        carry = lax.fori_loop(0, k_ref.shape[2], body, carry)
    _, l, acc = carry
    o_ref[0] = (acc / l).astype(o_ref.dtype)


def _attention(qt, sources, tq):
    b, _, _, n = qt.shape
    in_specs = [pl.BlockSpec((1, 1, HEAD_PAD, tq), lambda bi, h, i: (bi, h, 0, i))]
    args = [qt]
    for k, vt in sources:
        nt = k.shape[2]
        in_specs.append(pl.BlockSpec((1, 1, nt, KEY_TILE, HEAD_PAD), lambda bi, h, i: (bi, h, 0, 0, 0)))
        in_specs.append(pl.BlockSpec((1, 1, nt, V_HEAD, KEY_TILE), lambda bi, h, i: (bi, h, 0, 0, 0)))
        args += [k, vt]
    return pl.pallas_call(
        functools.partial(_attn_kernel, n_src=len(sources)),
        out_shape=jax.ShapeDtypeStruct((b, MLA_WIDTH, n), BF16),
        grid=(b, HEADS, n // tq),
        in_specs=in_specs,
        out_specs=pl.BlockSpec((1, V_HEAD, tq), lambda bi, h, i: (bi, h, i)),
        compiler_params=_cparams(("parallel", "parallel", "parallel")),
        name="attention",
    )(*args)


def _merge_kernel(x_ref, g1n_ref, sh1_ref, sc1_ref, gt1_ref, wg_ref, bg_ref, yps_ref, yat_ref,
                  wbp_ref, wbs_ref, wbm_ref, wout_ref, g2n_ref, sh2_ref, sc2_ref, rw_ref,
                  xmid_ref, h2_ref, lg_ref):
    x = x_ref[0]
    d = x.shape[1]
    h = _norm_mod(x, g1n_ref[...], sh1_ref[0], sc1_ref[0]).astype(BF16)
    yps = yps_ref[0]
    branches = (
        jnp.dot(yps[:, 0:POOL_WIDTH], wbp_ref[...], preferred_element_type=F32),
        jnp.dot(yps[:, POOL_WIDTH:POOL_WIDTH + SGU_WIDTH], wbs_ref[...], preferred_element_type=F32),
        lax.dot_general(yat_ref[0], wbm_ref[...], (((0,), (0,)), ((), ())),
                        preferred_element_type=F32),
    )
    merged = None
    for i in range(N_BRANCH):
        gl = jnp.dot(h, wg_ref[:, i * d:(i + 1) * d], preferred_element_type=F32) + bg_ref[:, i * d:(i + 1) * d]
        term = jax.nn.sigmoid(gl) * branches[i]
        merged = term if merged is None else merged + term
    out = jnp.dot(merged.astype(BF16), wout_ref[...], preferred_element_type=F32)
    xm = x + gt1_ref[0] * out
    xmid_ref[0] = xm
    h2 = _norm_mod(xm, g2n_ref[...], sh2_ref[0], sc2_ref[0])
    h2_ref[0] = h2
    hi = h2.astype(BF16)
    lo = (h2 - hi.astype(F32)).astype(BF16)
    r_hi = jnp.dot(hi, rw_ref[...], preferred_element_type=F32)
    r_lo = jnp.dot(lo, rw_ref[...], preferred_element_type=F32)
    lg_ref[0] = (r_hi[:, 0:N_EXPERTS] + r_hi[:, N_EXPERTS:2 * N_EXPERTS]) + r_lo[:, 0:N_EXPERTS]


def _merge(x, g1n, sh1, sc1, gt1, wg, bg, yps, yat, wbp, wbs, wbm, wout, g2n, sh2, sc2, rw, tm):
    b, n, d = x.shape
    row = lambda: pl.BlockSpec((1, 1, d), lambda bi, i: (bi, 0, 0))
    full = lambda shape: pl.BlockSpec(shape, lambda bi, i: (0,) * len(shape))
    return pl.pallas_call(
        _merge_kernel,
        out_shape=(jax.ShapeDtypeStruct((b, n, d), F32),
                   jax.ShapeDtypeStruct((b, n, d), F32),
                   jax.ShapeDtypeStruct((b, n, N_EXPERTS), F32)),
        grid=(b, n // tm),
        in_specs=[
            pl.BlockSpec((1, tm, d), lambda bi, i: (bi, i, 0)),
            full((1, d)), row(), row(), row(),
            full((d, N_BRANCH * d)), full((1, N_BRANCH * d)),
            pl.BlockSpec((1, tm, POOL_WIDTH + SGU_WIDTH), lambda bi, i: (bi, i, 0)),
            pl.BlockSpec((1, MLA_WIDTH, tm), lambda bi, i: (bi, 0, i)),
            full((POOL_WIDTH, d)), full((SGU_WIDTH, d)), full((MLA_WIDTH, d)), full((d, d)),
            full((1, d)), row(), row(),
            full((d, V7X_LANES)),
        ],
        out_specs=(pl.BlockSpec((1, tm, d), lambda bi, i: (bi, i, 0)),
                   pl.BlockSpec((1, tm, d), lambda bi, i: (bi, i, 0)),
                   pl.BlockSpec((1, tm, N_EXPERTS), lambda bi, i: (bi, i, 0))),
        compiler_params=_cparams(("parallel", "parallel")),
        name="merge",
    )(x, g1n, sh1, sc1, gt1, wg, bg, yps, yat, wbp, wbs, wbm, wout, g2n, sh2, sc2, rw)


def _route_kernel(lg_ref, rb_ref, cls_ref, w_ref):
    scores = jax.nn.sigmoid(lg_ref[...])
    sel = scores + rb_ref[...]
    best = None
    for g in range(N_GROUPS):
        rows = [sel[g * EXPERTS_PER_GROUP + j:g * EXPERTS_PER_GROUP + j + 1] for j in range(EXPERTS_PER_GROUP)]
        srow = [scores[g * EXPERTS_PER_GROUP + j:g * EXPERTS_PER_GROUP + j + 1] for j in range(EXPERTS_PER_GROUP)]
        v1, i1, s1 = rows[0], jnp.zeros_like(rows[0], jnp.int32), srow[0]
        for j in range(1, EXPERTS_PER_GROUP):
            gt = rows[j] > v1
            v1 = jnp.where(gt, rows[j], v1)
            i1 = jnp.where(gt, j, i1)
            s1 = jnp.where(gt, srow[j], s1)
        v2 = jnp.full_like(v1, -jnp.inf)
        i2 = jnp.zeros_like(i1)
        s2 = jnp.zeros_like(s1)
        for j in range(EXPERTS_PER_GROUP):
            gt = (i1 != j) & (rows[j] > v2)
            v2 = jnp.where(gt, rows[j], v2)
            i2 = jnp.where(gt, j, i2)
            s2 = jnp.where(gt, srow[j], s2)
        gs = v1 + v2
        if best is None:
            best = (gs, jnp.full_like(i1, g), i1, i2, s1, s2)
        else:
            gt = gs > best[0]
            cand = (gs, jnp.full_like(i1, g), i1, i2, s1, s2)
            best = tuple(jnp.where(gt, c, b) for c, b in zip(cand, best))
    _, grp, i1, i2, s1, s2 = best
    lo = jnp.minimum(i1, i2)
    hi = jnp.maximum(i1, i2)
    base = jnp.where(lo == 0, 0, jnp.where(lo == 1, 3, 5))
    cls_ref[...] = grp * PAIRS_PER_GROUP + base + (hi - lo - 1)
    tot = s1 + s2
    first_is_lo = i1 < i2
    w_ref[0:1] = jnp.where(first_is_lo, s1, s2) / tot
    w_ref[1:2] = jnp.where(first_is_lo, s2, s1) / tot


def _route(logits_t, rb_t, tn):
    _, t = logits_t.shape
    return pl.pallas_call(
        _route_kernel,
        out_shape=(jax.ShapeDtypeStruct((1, t), jnp.int32), jax.ShapeDtypeStruct((2, t), F32)),
        grid=(t // tn,),
        in_specs=[pl.BlockSpec((N_EXPERTS, tn), lambda i: (0, i)),
                  pl.BlockSpec((N_EXPERTS, tn), lambda i: (0, 0))],
        out_specs=(pl.BlockSpec((1, tn), lambda i: (0, i)), pl.BlockSpec((2, tn), lambda i: (0, i))),
        compiler_params=_cparams(("parallel",)),
        name="route",
    )(logits_t, rb_t)


def _moe_kernel(elo_ref, ehi_ref, nused_ref, idx_ref, h2_hbm, wab_ref, wgl_ref, wul_ref, wdl_ref,
                wgh_ref, wuh_ref, wdh_ref, y_ref, xbuf, sem, *, tmo):
    i = pl.program_id(0)

    @pl.when(i < nused_ref[0])
    def _():
        base = i * tmo

        def issue(r, c):
            tok = idx_ref[base + r]
            pltpu.make_async_copy(h2_hbm.at[pl.ds(tok, 1)], xbuf.at[pl.ds(r, 1)], sem).start()
            return c

        lax.fori_loop(0, tmo, issue, 0)
        pltpu.make_async_copy(h2_hbm.at[pl.ds(0, tmo)], xbuf, sem).wait()
        xs = xbuf[...].astype(BF16)
        wab = wab_ref[...]

        def ffn(wg_ref, wu_ref, col):
            gate = jnp.dot(xs, wg_ref[0], preferred_element_type=F32)
            up = jnp.dot(xs, wu_ref[0], preferred_element_type=F32)
            return ((gate * jax.nn.sigmoid(gate)) * up * wab[:, col:col + 1]).astype(BF16)

        y = jnp.dot(ffn(wgl_ref, wul_ref, 0), wdl_ref[0], preferred_element_type=F32)
        y = y + jnp.dot(ffn(wgh_ref, wuh_ref, 1), wdh_ref[0], preferred_element_type=F32)
        y_ref[...] = y

    @pl.when(i >= nused_ref[0])
    def _():
        y_ref[...] = jnp.zeros_like(y_ref)


def _moe(tile_lo, tile_hi, n_used, src_idx, h2, wab, wg, wu, wd, tmo):
    t, d = h2.shape
    n_tiles = tile_lo.shape[0]
    de = wg.shape[2]
    lo3 = lambda i, elo, ehi, nu, idx: (elo[i], 0, 0)
    hi3 = lambda i, elo, ehi, nu, idx: (ehi[i], 0, 0)
    grid_spec = pltpu.PrefetchScalarGridSpec(
        num_scalar_prefetch=4,
        grid=(n_tiles,),
        in_specs=[
            pl.BlockSpec(memory_space=pl.ANY),
            pl.BlockSpec((tmo, 2), lambda i, elo, ehi, nu, idx: (i, 0)),
            pl.BlockSpec((1, d, de), lo3), pl.BlockSpec((1, d, de), lo3), pl.BlockSpec((1, de, d), lo3),
            pl.BlockSpec((1, d, de), hi3), pl.BlockSpec((1, d, de), hi3), pl.BlockSpec((1, de, d), hi3),
        ],
        out_specs=pl.BlockSpec((tmo, d), lambda i, elo, ehi, nu, idx: (i, 0)),
        scratch_shapes=[pltpu.VMEM((tmo, d), F32), pltpu.SemaphoreType.DMA],
    )
    return pl.pallas_call(
        functools.partial(_moe_kernel, tmo=tmo),
        out_shape=jax.ShapeDtypeStruct((n_tiles * tmo, d), F32),
        grid_spec=grid_spec,
        compiler_params=_cparams(("arbitrary",)),
        name="moe",
    )(tile_lo, tile_hi, n_used, src_idx, h2, wab, wg, wu, wd, wg, wu, wd)


def _combine_kernel(pos_ref, x_ref, g_ref, y_hbm, o_ref, ybuf, sem, *, tmc):
    i = pl.program_id(0)
    base = i * tmc

    def issue(r, c):
        pltpu.make_async_copy(y_hbm.at[pl.ds(pos_ref[base + r], 1)], ybuf.at[pl.ds(r, 1)], sem).start()
        return c

    lax.fori_loop(0, tmc, issue, 0)
    pltpu.make_async_copy(y_hbm.at[pl.ds(0, tmc)], ybuf, sem).wait()
    o_ref[...] = x_ref[...] + g_ref[0] * ybuf[...]


def _combine(pos, x2d, gate, y_sorted, n, tmc):
    t, d = x2d.shape
    per_batch = n // tmc
    grid_spec = pltpu.PrefetchScalarGridSpec(
        num_scalar_prefetch=1,
        grid=(t // tmc,),
        in_specs=[
            pl.BlockSpec((tmc, d), lambda i, pos: (i, 0)),
            pl.BlockSpec((1, 1, d), lambda i, pos: (i // per_batch, 0, 0)),
            pl.BlockSpec(memory_space=pl.ANY),
        ],
        out_specs=pl.BlockSpec((tmc, d), lambda i, pos: (i, 0)),
        scratch_shapes=[pltpu.VMEM((tmc, d), F32), pltpu.SemaphoreType.DMA],
    )
    return pl.pallas_call(
        functools.partial(_combine_kernel, tmc=tmc),
        out_shape=jax.ShapeDtypeStruct((t, d), F32),
        grid_spec=grid_spec,
        compiler_params=_cparams(("arbitrary",)),
        name="combine",
    )(pos, x2d, gate, y_sorted)


def _plan(cls, tmo):
    t = cls.shape[0]
    n_tiles = t // tmo + N_CLASSES
    order = jnp.argsort(cls, stable=True).astype(jnp.int32)
    counts = jnp.sum(cls[None, :] == jnp.arange(N_CLASSES, dtype=jnp.int32)[:, None], axis=1).astype(jnp.int32)
    tiles = (counts + tmo - 1) // tmo
    tile_end = jnp.cumsum(tiles)
    tile_start = tile_end - tiles
    tok_start = jnp.cumsum(counts) - counts
    n_used = tile_end[-1]
    tile_id = jnp.arange(n_tiles, dtype=jnp.int32)
    tile_cls = jnp.sum(tile_id[:, None] >= tile_end[None, :], axis=1).astype(jnp.int32)
    last_cls = jnp.max(jnp.where(counts > 0, jnp.arange(N_CLASSES, dtype=jnp.int32), 0))
    tile_cls = jnp.where(tile_id < n_used, tile_cls, last_cls)
    grp = tile_cls // PAIRS_PER_GROUP
    pair = tile_cls % PAIRS_PER_GROUP
    lo = jnp.where(pair < 3, 0, jnp.where(pair < 5, 1, 2))
    hi = jnp.where(pair < 3, pair + 1, jnp.where(pair < 5, pair - 1, 3))
    tile_lo = (grp * EXPERTS_PER_GROUP + lo).astype(jnp.int32)
    tile_hi = (grp * EXPERTS_PER_GROUP + hi).astype(jnp.int32)
    row = jnp.arange(n_tiles * tmo, dtype=jnp.int32)
    row_cls = jnp.repeat(tile_cls, tmo)
    rank = row - jnp.repeat(tile_start[tile_cls] * tmo, tmo)
    valid = (rank < counts[row_cls]) & (jnp.repeat(tile_id, tmo) < n_used)
    src = order[jnp.clip(tok_start[row_cls] + jnp.where(valid, rank, 0), 0, t - 1)]
    rank_tok = jnp.zeros((t,), jnp.int32).at[order].set(jnp.arange(t, dtype=jnp.int32)) - tok_start[cls]
    pos = tile_start[cls] * tmo + rank_tok
    return tile_lo, tile_hi, n_used.reshape(1).astype(jnp.int32), src.astype(jnp.int32), pos.astype(jnp.int32)


def _tiles(n):
    big = 512 if n % 512 == 0 else 256
    return dict(inproj=big, poolsgu=big, qkv=256, attn=big, merge=256)


def _project(x, lw, mod, tl):
    g1n, sh1, sc1 = lw["norm1_g"], mod[0], mod[1]
    puv, qkt = _inproj(x, g1n, sh1, sc1, lw["w_puv"], lw["w_qkt"], tl["inproj"])
    return puv, qkt


def _keys(qkt, lw, rope, tl):
    tm = tl["qkv"]
    cos_t, sin_t = rope if rope is not None else (lw["ones_t"], lw["ones_t"])
    bro = lambda g: jnp.broadcast_to(g[:, None], (g.shape[0], tm))
    return _qkv(qkt, bro(lw["qa_norm_g"]), bro(lw["kva_norm_g"]), bro(lw["q_norm_g"]), bro(lw["k_norm_g"]),
                cos_t, sin_t, lw["w_uq_t"], lw["w_ukv_t"], rope is not None, tm)


def _mixer_and_moe(x, puv, qt, sources, lw, mod, rw, rb, tl):
    b, n, d = x.shape
    sh1, sc1, gt1, sh2, sc2, gt2 = mod
    yps = _poolsgu(puv, lw["pool_bd"], lw["pool_scale"], lw["sgu_norm_g"], lw["sgu_ws"], lw["sgu_bias"],
                   tl["poolsgu"])
    yat = _attention(qt, sources, tl["attn"])
    xmid, h2, logits = _merge(x, lw["norm1_g"], sh1, sc1, gt1, lw["w_gate"], lw["b_gate"], yps, yat,
                              lw["w_br_pool"], lw["w_br_sgu"], lw["w_br_mla"], lw["w_out"],
                              lw["norm2_g"], sh2, sc2, rw, tl["merge"])
    t = b * n
    tn = next(w for w in (2048, 1024, 512, 256) if t % w == 0)
    cls, w2 = _route(logits.reshape(t, N_EXPERTS).T, jnp.broadcast_to(rb[:, None], (N_EXPERTS, tn)), tn)
    tile_lo, tile_hi, n_used, src, pos = _plan(cls[0], MOE_TILE)
    wab = w2.T[src]
    y_sorted = _moe(tile_lo, tile_hi, n_used, src, h2.reshape(t, d), wab,
                    lw["w_e_gate"], lw["w_e_up"], lw["w_e_down"], MOE_TILE)
    out = _combine(pos, xmid.reshape(t, d), gt2, y_sorted, n, MOE_TILE)
    return out.reshape(b, n, d)


def _rope_tables_t(n):
    rows = n // GRID_W
    row = jnp.repeat(jnp.arange(rows, dtype=F32), GRID_W)
    col = jnp.tile(jnp.arange(GRID_W, dtype=F32), rows)
    inv = ROPE_BASE ** (-(jnp.arange(ROPE_PAIRS, dtype=F32) * 2.0 / ROPE_AXIS))
    ang_r = inv[:, None] * row[None, :]
    ang_c = inv[:, None] * col[None, :]
    ang = jnp.concatenate([ang_r, ang_r, ang_c, ang_c], axis=0)
    return jnp.cos(ang), jnp.sin(ang)


def kernel(x, c, ctx, c_ctx, w_mod, b_mod, norm1_g, norm2_g, w_in, pool_w, pool_scale, sgu_norm_g, sgu_ws, sgu_b, qa_norm_g, w_uq, kva_norm_g, w_ukv, q_norm_g, k_norm_g, w_br_pool, w_br_sgu, w_br_mla, b_gate, w_out, router_w, router_b, w_e_gate, w_e_up, w_e_down):
    bsz, n_lat, d = x.shape
    n_ctx = ctx.shape[1]
    depth = w_mod.shape[0]
    assert bsz + 1 <= V7X_SUBLANES

    cond8 = jnp.zeros((V7X_SUBLANES, d), F32).at[:bsz].set(c).at[bsz].set(c_ctx)
    mods = _adaln(cond8, w_mod, b_mod)
    rope = _rope_tables_t(n_lat)
    rw_hi = router_w.astype(BF16)
    rw_lo = (router_w - rw_hi.astype(F32)).astype(BF16)
    rw = jnp.concatenate([rw_hi, rw_lo, jnp.zeros((d, V7X_LANES - 2 * N_EXPERTS), BF16)], axis=1)
    tl_lat, tl_ctx = _tiles(n_lat), _tiles(n_ctx)

    x_lat, x_ctx = x, ctx
    for l in range(depth):
        last = l == depth - 1
        wl = w_in[l]
        lw = dict(
            norm1_g=norm1_g[l][None], norm2_g=norm2_g[l][None],
            w_puv=wl[:, :PUV_WIDTH].astype(BF16),
            w_qkt=wl[:, OFF_QA:OFF_GATE].T.astype(BF16),
            w_gate=wl[:, OFF_GATE:].astype(BF16), b_gate=b_gate[l][None],
            pool_bd=jax.scipy.linalg.block_diag(*[pool_w[l, g] for g in range(len(POOL_WINDOWS))]).astype(BF16),
            pool_scale=pool_scale[l][None], sgu_norm_g=sgu_norm_g[l][None],
            sgu_ws=sgu_ws[l].astype(BF16), sgu_bias=jnp.repeat(sgu_b[l].T, SGU_GROUP, axis=1),
            qa_norm_g=qa_norm_g[l], kva_norm_g=kva_norm_g[l], q_norm_g=q_norm_g[l], k_norm_g=k_norm_g[l],
            w_uq_t=w_uq[l].T.astype(BF16), w_ukv_t=w_ukv[l].T.astype(BF16),
            w_br_pool=w_br_pool[l].astype(BF16), w_br_sgu=w_br_sgu[l].astype(BF16),
            w_br_mla=w_br_mla[l].astype(BF16), w_out=w_out[l].astype(BF16),
            w_e_gate=w_e_gate[l].astype(BF16), w_e_up=w_e_up[l].astype(BF16), w_e_down=w_e_down[l].astype(BF16),
            ones_t=jnp.ones((QK_ROPE, n_ctx), F32),
        )
        m = mods[l]
        mod_lat = [m[:bsz, i * d:(i + 1) * d][:, None, :] for i in range(6)]
        mod_ctx = [jnp.broadcast_to(m[bsz:bsz + 1, i * d:(i + 1) * d][:, None, :], (bsz, 1, d)) for i in range(6)]

        puv_c, qkt_c = _project(x_ctx, lw, mod_ctx, tl_ctx)
        qt_c, k_c, vt_c = _keys(qkt_c, lw, None, tl_ctx)
        puv, qkt = _project(x_lat, lw, mod_lat, tl_lat)
        qt, k_l, vt_l = _keys(qkt, lw, rope, tl_lat)
        x_lat = _mixer_and_moe(x_lat, puv, qt, [(k_l, vt_l), (k_c, vt_c)], lw, mod_lat, rw, router_b, tl_lat)
        if not last:
            x_ctx = _mixer_and_moe(x_ctx, puv_c, qt_c, [(k_c, vt_c)], lw, mod_ctx, rw, router_b, tl_ctx)
    return x_lat
```

```python
import functools
import math

import jax
import jax.numpy as jnp
from jax import lax
from jax.experimental import pallas as pl
from jax.experimental.pallas import tpu as pltpu

F32 = jnp.float32
BF16 = jnp.bfloat16

GRID_W = 64
POOL_WINDOWS = (2, 4, 8, 16)
POOL_WIDTH = 256
POOL_GROUP = POOL_WIDTH // len(POOL_WINDOWS)
SGU_WIDTH = 256
SGU_GROUPS = 4
SGU_GROUP = SGU_WIDTH // SGU_GROUPS
CHUNK = 128
HEADS = 8
QK_NOPE = 64
QK_ROPE = 32
V_HEAD = 64
Q_LORA = 384
KV_LORA = 256
QK_HEAD = QK_NOPE + QK_ROPE
MLA_WIDTH = HEADS * V_HEAD
SM_SCALE = QK_HEAD ** -0.5
ROPE_BASE = 10000.0
ROPE_AXIS = QK_ROPE // 2
ROPE_PAIRS = ROPE_AXIS // 2
N_BRANCH = 3
OFF_U = POOL_WIDTH
OFF_V = OFF_U + SGU_WIDTH
OFF_QA = OFF_V + SGU_WIDTH
OFF_KVA = OFF_QA + Q_LORA
OFF_KR = OFF_KVA + KV_LORA
OFF_GATE = OFF_KR + QK_ROPE
N_EXPERTS = 16
N_GROUPS = 4
EXPERTS_PER_GROUP = N_EXPERTS // N_GROUPS
PAIRS_PER_GROUP = EXPERTS_PER_GROUP * (EXPERTS_PER_GROUP - 1) // 2
N_CLASSES = N_GROUPS * PAIRS_PER_GROUP
EPS = 1e-6

PUV_WIDTH = OFF_QA
QKT_ROWS = OFF_GATE - OFF_QA
HEAD_PAD = 128
QK_EXP2_SCALE = SM_SCALE * math.log2(math.e)

V7X_LANES = 128
V7X_SUBLANES = 8
V7X_VMEM_LIMIT = 52 * 1024 * 1024
KEY_TILE = 256
MOE_TILE = 256


def _cparams(sem, vmem=V7X_VMEM_LIMIT):
    return pltpu.CompilerParams(dimension_semantics=sem, vmem_limit_bytes=vmem)


def _norm_mod(x, g, shift, scale):
    ms = jnp.mean(x * x, axis=-1, keepdims=True)
    y = x * lax.rsqrt(ms + EPS) * g
    return y * (1.0 + scale) + shift


def _rms_rows(x, g):
    ms = jnp.mean(x * x, axis=0, keepdims=True)
    return x * lax.rsqrt(ms + EPS) * g


def _adaln_kernel(c_ref, w_ref, b_ref, o_ref):
    c = c_ref[...]
    s = (c * jax.nn.sigmoid(c)).astype(BF16)
    o_ref[0] = jnp.dot(s, w_ref[0].astype(BF16), preferred_element_type=F32) + b_ref[0]


def _adaln(cond8, w_mod, b_mod):
    depth, d, n6 = w_mod.shape
    tn = 1536
    return pl.pallas_call(
        _adaln_kernel,
        out_shape=jax.ShapeDtypeStruct((depth, V7X_SUBLANES, n6), F32),
        grid=(depth, n6 // tn),
        in_specs=[
            pl.BlockSpec((V7X_SUBLANES, d), lambda l, j: (0, 0)),
            pl.BlockSpec((1, d, tn), lambda l, j: (l, 0, j)),
            pl.BlockSpec((1, 1, tn), lambda l, j: (l, 0, j)),
        ],
        out_specs=pl.BlockSpec((1, V7X_SUBLANES, tn), lambda l, j: (l, 0, j)),
        compiler_params=_cparams(("parallel", "parallel")),
        name="adaln",
    )(cond8, w_mod, b_mod.reshape(depth, 1, n6))


def _inproj_kernel(x_ref, g_ref, sh_ref, sc_ref, wn_ref, wt_ref, puv_ref, qkt_ref):
    h = _norm_mod(x_ref[0], g_ref[...], sh_ref[0], sc_ref[0]).astype(BF16)
    puv_ref[0] = jnp.dot(h, wn_ref[...], preferred_element_type=F32)
    qkt_ref[0] = lax.dot_general(wt_ref[...], h, (((1,), (1,)), ((), ())),
                                 preferred_element_type=F32)


def _inproj(x, g, shift, scale, w_nat, w_t, tm):
    b, n, d = x.shape
    return pl.pallas_call(
        _inproj_kernel,
        out_shape=(jax.ShapeDtypeStruct((b, n, PUV_WIDTH), F32),
                   jax.ShapeDtypeStruct((b, QKT_ROWS, n), F32)),
        grid=(b, n // tm),
        in_specs=[
            pl.BlockSpec((1, tm, d), lambda bi, i: (bi, i, 0)),
            pl.BlockSpec((1, d), lambda bi, i: (0, 0)),
            pl.BlockSpec((1, 1, d), lambda bi, i: (bi, 0, 0)),
            pl.BlockSpec((1, 1, d), lambda bi, i: (bi, 0, 0)),
            pl.BlockSpec((d, PUV_WIDTH), lambda bi, i: (0, 0)),
            pl.BlockSpec((QKT_ROWS, d), lambda bi, i: (0, 0)),
        ],
        out_specs=(pl.BlockSpec((1, tm, PUV_WIDTH), lambda bi, i: (bi, i, 0)),
                   pl.BlockSpec((1, QKT_ROWS, tm), lambda bi, i: (bi, 0, i))),
        compiler_params=_cparams(("parallel", "parallel")),
        name="inproj",
    )(x, g, shift, scale, w_nat, w_t)


POOL_HALO = 8


def _poolsgu_kernel(puv_ref, prev_ref, next_ref, poolw_ref, pscale_ref, sgug_ref, ws_ref,
                    bias_ref, o_ref, *, n, tc):
    i = pl.program_id(1)
    last = pl.num_programs(1) - 1
    puv = puv_ref[0]
    p = puv[:, :POOL_WIDTH]
    prev = jnp.where(i > 0, prev_ref[0], 0.0)
    nxt = jnp.where(i < last, next_ref[0], 0.0)
    e = jnp.concatenate([prev, p, nxt], axis=0)
    rows = tc + 2 * POOL_HALO
    a2 = e + pltpu.roll(e, 1, 0)
    a4 = a2 + pltpu.roll(a2, 2, 0)
    a8 = a4 + pltpu.roll(a4, 4, 0)
    a16 = a8 + pltpu.roll(a8, 8, 0)
    w2 = a2[POOL_HALO:POOL_HALO + tc]
    w4 = pltpu.roll(a4, rows - 1, 0)[POOL_HALO:POOL_HALO + tc]
    w8 = pltpu.roll(a8, rows - 3, 0)[POOL_HALO:POOL_HALO + tc]
    w16 = pltpu.roll(a16, rows - 7, 0)[POOL_HALO:POOL_HALO + tc]
    lane = lax.broadcasted_iota(jnp.int32, (tc, POOL_WIDTH), 1)
    t = i * tc + lax.broadcasted_iota(jnp.int32, (tc, POOL_WIDTH), 0)
    grp = lane // POOL_GROUP
    half = jnp.where(grp == 0, 1, jnp.where(grp == 1, 2, jnp.where(grp == 2, 4, 8)))
    cnt = (jnp.minimum(t + half, n) - jnp.maximum(t - half, 0)).astype(F32)
    wsum = jnp.where(grp == 0, w2, jnp.where(grp == 1, w4, jnp.where(grp == 2, w8, w16)))
    dlt = (wsum / cnt - p).astype(BF16)
    y_pool = jnp.dot(dlt, poolw_ref[...], preferred_element_type=F32) * pscale_ref[...]
    o_ref[0, :, 0:POOL_WIDTH] = y_pool.astype(o_ref.dtype)

    u = jax.nn.gelu(puv[:, OFF_U:OFF_V])
    gv = jax.nn.gelu(puv[:, OFF_V:OFF_QA])
    vn = gv * lax.rsqrt(jnp.mean(gv * gv, axis=-1, keepdims=True) + EPS) * sgug_ref[...]
    clane = lax.broadcasted_iota(jnp.int32, (CHUNK, SGU_WIDTH), 1) // SGU_GROUP
    for c in range(tc // CHUNK):
        vc = vn[c * CHUNK:(c + 1) * CHUNK]
        mixed = bias_ref[...]
        for gi in range(SGU_GROUPS):
            vm = jnp.where(clane == gi, vc, 0.0).astype(BF16)
            mixed = mixed + jnp.dot(ws_ref[gi], vm, preferred_element_type=F32)
        y = u[c * CHUNK:(c + 1) * CHUNK] * mixed
        o_ref[0, c * CHUNK:(c + 1) * CHUNK, POOL_WIDTH:POOL_WIDTH + SGU_WIDTH] = y.astype(o_ref.dtype)


def _poolsgu(puv, poolw, pscale, sgug, ws, bias, tc):
    b, n, _ = puv.shape
    nh = n // POOL_HALO
    per = tc // POOL_HALO
    kern = functools.partial(_poolsgu_kernel, n=n, tc=tc)
    return pl.pallas_call(
        kern,
        out_shape=jax.ShapeDtypeStruct((b, n, POOL_WIDTH + SGU_WIDTH), BF16),
        grid=(b, n // tc),
        in_specs=[
            pl.BlockSpec((1, tc, PUV_WIDTH), lambda bi, i: (bi, i, 0)),
            pl.BlockSpec((1, POOL_HALO, POOL_WIDTH),
                         lambda bi, i: (bi, jnp.maximum(i * per - 1, 0), 0)),
            pl.BlockSpec((1, POOL_HALO, POOL_WIDTH),
                         lambda bi, i: (bi, jnp.minimum((i + 1) * per, nh - 1), 0)),
            pl.BlockSpec((POOL_WIDTH, POOL_WIDTH), lambda bi, i: (0, 0)),
            pl.BlockSpec((1, POOL_WIDTH), lambda bi, i: (0, 0)),
            pl.BlockSpec((1, SGU_WIDTH), lambda bi, i: (0, 0)),
            pl.BlockSpec((SGU_GROUPS, CHUNK, CHUNK), lambda bi, i: (0, 0, 0)),
            pl.BlockSpec((CHUNK, SGU_WIDTH), lambda bi, i: (0, 0)),
        ],
        out_specs=pl.BlockSpec((1, tc, POOL_WIDTH + SGU_WIDTH), lambda bi, i: (bi, i, 0)),
        compiler_params=_cparams(("parallel", "parallel")),
        name="poolsgu",
    )(puv, puv, puv, poolw, pscale, sgug, ws, bias)


V_ROWS = 80


def _qkv_kernel(lat_ref, ctx_ref, gqa_ref, gkva_ref, gq_ref, gk_ref, cos_ref, sin_ref, wq_ref, wkv_ref,
                qt_ref, k_ref, vt_ref, *, n_lat_tiles):
    tm = KEY_TILE
    blk = jnp.where(pl.program_id(1) < n_lat_tiles, lat_ref[0], ctx_ref[0])
    qa_n = _rms_rows(blk[0:Q_LORA], gqa_ref[...]).astype(BF16)
    kva_n = _rms_rows(blk[Q_LORA:Q_LORA + KV_LORA], gkva_ref[...]).astype(BF16)
    q = jnp.dot(wq_ref[...], qa_n, preferred_element_type=F32)
    kv = jnp.dot(wkv_ref[...], kva_n, preferred_element_type=F32)
    gq = gq_ref[...]
    gk = gk_ref[...]

    def rope(x):
        r = ROPE_PAIRS
        rot = jnp.concatenate([-x[r:2 * r], x[0:r], -x[3 * r:4 * r], x[2 * r:3 * r]], axis=0)
        return x * cos_ref[...] + rot * sin_ref[...]

    k_rope = rope(_rms_rows(blk[Q_LORA + KV_LORA:QKT_ROWS], gk[QK_NOPE:QK_HEAD]))
    pad = jnp.zeros((HEAD_PAD - QK_HEAD, tm), F32)
    ones_rows = jnp.where(lax.broadcasted_iota(jnp.int32, (V_ROWS - V_HEAD, tm), 0) == 0, 1.0, 0.0)
    kv_rows = QK_NOPE + V_HEAD
    for h in range(HEADS):
        qh = q[h * QK_HEAD:(h + 1) * QK_HEAD]
        q_nope = _rms_rows(qh[0:QK_NOPE], gq[0:QK_NOPE])
        q_rope = rope(_rms_rows(qh[QK_NOPE:QK_HEAD], gq[QK_NOPE:QK_HEAD]))
        qcat = jnp.concatenate([q_nope, q_rope, pad], axis=0) * QK_EXP2_SCALE
        qt_ref[0, h] = qcat.astype(qt_ref.dtype)
        k_nope = _rms_rows(kv[h * kv_rows:h * kv_rows + QK_NOPE], gk[0:QK_NOPE])
        kcat = jnp.concatenate([k_nope, k_rope, pad], axis=0)
        k_ref[0, h, 0] = kcat.T.astype(k_ref.dtype)
        v = jnp.concatenate([kv[h * kv_rows + QK_NOPE:(h + 1) * kv_rows], ones_rows], axis=0)
        vt_ref[0, h, 0] = v.astype(vt_ref.dtype)


def _qkv(qkt_lat, qkt_ctx, gqa, gkva, gq, gk, cos_t, sin_t, wq_t, wkv_t):
    b, _, n_lat = qkt_lat.shape
    n_ctx = qkt_ctx.shape[2]
    tm = KEY_TILE
    nl, nc = n_lat // tm, n_ctx // tm
    nt = nl + nc
    kern = functools.partial(_qkv_kernel, n_lat_tiles=nl)
    full = lambda shape: pl.BlockSpec(shape, lambda bi, i: (0,) * len(shape))
    return pl.pallas_call(
        kern,
        out_shape=(jax.ShapeDtypeStruct((b, HEADS, HEAD_PAD, nt * tm), BF16),
                   jax.ShapeDtypeStruct((b, HEADS, nt, tm, HEAD_PAD), BF16),
                   jax.ShapeDtypeStruct((b, HEADS, nt, V_ROWS, tm), BF16)),
        grid=(b, nt),
        in_specs=[
            pl.BlockSpec((1, QKT_ROWS, tm), lambda bi, i: (bi, 0, jnp.minimum(i, nl - 1))),
            pl.BlockSpec((1, QKT_ROWS, tm), lambda bi, i: (bi, 0, jnp.maximum(i - nl, 0))),
            full((Q_LORA, tm)), full((KV_LORA, tm)), full((QK_HEAD, tm)), full((QK_HEAD, tm)),
            pl.BlockSpec((QK_ROPE, tm), lambda bi, i: (0, i)),
            pl.BlockSpec((QK_ROPE, tm), lambda bi, i: (0, i)),
            full((HEADS * QK_HEAD, Q_LORA)), full((HEADS * (QK_NOPE + V_HEAD), KV_LORA)),
        ],
        out_specs=(pl.BlockSpec((1, HEADS, HEAD_PAD, tm), lambda bi, i: (bi, 0, 0, i)),
                   pl.BlockSpec((1, HEADS, 1, tm, HEAD_PAD), lambda bi, i: (bi, 0, i, 0, 0)),
                   pl.BlockSpec((1, HEADS, 1, V_ROWS, tm), lambda bi, i: (bi, 0, i, 0, 0))),
        compiler_params=_cparams(("parallel", "parallel")),
        name="qkv",
    )(qkt_lat, qkt_ctx, gqa, gkva, gq, gk, cos_t, sin_t, wq_t, wkv_t)


def _attn_kernel(qt_ref, k_ref, vt_ref, o_ref, m_ref, acc_ref, s_ref, *, g_n, s_n, per):
    m_ref[...] = jnp.full(m_ref.shape, -jnp.inf, F32)
    acc_ref[...] = jnp.zeros(acc_ref.shape, F32)

    def qk(g, s, j, slot):
        s_ref[slot, g * s_n + s] = jnp.dot(k_ref[0, g, s * per + j], qt_ref[0, g],
                                            preferred_element_type=F32)

    def soft_pv(g, s, j, slot):
        ci = g * s_n + s
        sc = s_ref[slot, ci]
        m = m_ref[ci]
        m_new = jnp.maximum(m, jnp.max(sc, axis=0, keepdims=True))
        alpha = jnp.exp2(m - m_new)
        p = jnp.exp2(sc - m_new).astype(BF16)
        acc_ref[ci] = alpha * acc_ref[ci] + jnp.dot(vt_ref[0, g, s * per + j], p,
                                                     preferred_element_type=F32)
        m_ref[ci] = m_new

    for g in range(g_n):
        for s in range(s_n):
            qk(g, s, 0, 0)

    def body(jj, c):
        for half in range(2):
            j = 2 * jj + half
            for g in range(g_n):
                for s in range(s_n):
                    qk(g, s, j + 1, 1 - half)
                    soft_pv(g, s, j, half)
        return c

    if per > 1:
        lax.fori_loop(0, (per - 1) // 2, body, 0)
    for g in range(g_n):
        for s in range(s_n):
            soft_pv(g, s, per - 1, (per - 1) % 2)
    for g in range(g_n):
        m = m_ref[g * s_n]
        for s in range(1, s_n):
            m = jnp.maximum(m, m_ref[g * s_n + s])
        tot = None
        for s in range(s_n):
            part = acc_ref[g * s_n + s] * jnp.exp2(m_ref[g * s_n + s] - m)
            tot = part if tot is None else tot + part
        o_ref[0, g * V_HEAD:(g + 1) * V_HEAD] = (tot[0:V_HEAD] / tot[V_HEAD:V_HEAD + 1]).astype(o_ref.dtype)


ATTN_HEADS_PER_STEP = 2


def _attention(qt, k, vt, q0, nq, tq, key0, nkeys):
    b = qt.shape[0]
    g_n = ATTN_HEADS_PER_STEP
    s_n = next(s for s in (3, 1) if nkeys % s == 0 and (nkeys // s - 1) % 2 == 0)
    assert key0 % nkeys == 0 and HEADS % g_n == 0
    kb = key0 // nkeys
    kern = functools.partial(_attn_kernel, g_n=g_n, s_n=s_n, per=nkeys // s_n)
    return pl.pallas_call(
        kern,
        out_shape=jax.ShapeDtypeStruct((b, MLA_WIDTH, nq * tq), BF16),
        grid=(b, HEADS // g_n, nq),
        in_specs=[
            pl.BlockSpec((1, g_n, HEAD_PAD, tq), lambda bi, hg, i: (bi, hg, 0, q0 + i)),
            pl.BlockSpec((1, g_n, nkeys, KEY_TILE, HEAD_PAD), lambda bi, hg, i: (bi, hg, kb, 0, 0)),
            pl.BlockSpec((1, g_n, nkeys, V_ROWS, KEY_TILE), lambda bi, hg, i: (bi, hg, kb, 0, 0)),
        ],
        out_specs=pl.BlockSpec((1, g_n * V_HEAD, tq), lambda bi, hg, i: (bi, hg, i)),
        scratch_shapes=[pltpu.VMEM((g_n * s_n, 1, tq), F32),
                        pltpu.VMEM((g_n * s_n, V_ROWS, tq), F32),
                        pltpu.VMEM((2, g_n * s_n, KEY_TILE, tq), F32)],
        compiler_params=_cparams(("parallel", "parallel", "parallel")),
        name="attention",
    )(qt, k, vt)


def _merge_kernel(x_ref, g1n_ref, sh1_ref, sc1_ref, gt1_ref, wg_ref, bg_ref, yps_ref, yat_ref,
                  wbp_ref, wbs_ref, wbm_ref, wout_ref, g2n_ref, sh2_ref, sc2_ref, rw_ref,
                  xmid_ref, h2_ref, lg_ref):
    x = x_ref[0]
    d = x.shape[1]
    h = _norm_mod(x, g1n_ref[...], sh1_ref[0], sc1_ref[0]).astype(BF16)
    yps = yps_ref[0]
    branches = (
        jnp.dot(yps[:, 0:POOL_WIDTH], wbp_ref[...], preferred_element_type=F32),
        jnp.dot(yps[:, POOL_WIDTH:POOL_WIDTH + SGU_WIDTH], wbs_ref[...], preferred_element_type=F32),
        lax.dot_general(yat_ref[0], wbm_ref[...], (((0,), (0,)), ((), ())),
                        preferred_element_type=F32),
    )
    merged = None
    for i in range(N_BRANCH):
        gl = jnp.dot(h, wg_ref[:, i * d:(i + 1) * d], preferred_element_type=F32) + bg_ref[:, i * d:(i + 1) * d]
        term = jax.nn.sigmoid(gl) * branches[i]
        merged = term if merged is None else merged + term
    out = jnp.dot(merged.astype(BF16), wout_ref[...], preferred_element_type=F32)
    xm = x + gt1_ref[0] * out
    xmid_ref[0] = xm
    h2 = _norm_mod(xm, g2n_ref[...], sh2_ref[0], sc2_ref[0])
    h2_ref[0] = h2
    hi = h2.astype(BF16)
    lo = (h2 - hi.astype(F32)).astype(BF16)
    r_hi = jnp.dot(hi, rw_ref[...], preferred_element_type=F32)
    r_lo = jnp.dot(lo, rw_ref[...], preferred_element_type=F32)
    lg_ref[0] = (r_hi[:, 0:N_EXPERTS] + r_hi[:, N_EXPERTS:2 * N_EXPERTS]) + r_lo[:, 0:N_EXPERTS]


def _merge(x, g1n, sh1, sc1, gt1, wg, bg, yps, yat, wbp, wbs, wbm, wout, g2n, sh2, sc2, rw, tm):
    b, n, d = x.shape
    row = lambda: pl.BlockSpec((1, 1, d), lambda bi, i: (bi, 0, 0))
    full = lambda shape: pl.BlockSpec(shape, lambda bi, i: (0,) * len(shape))
    return pl.pallas_call(
        _merge_kernel,
        out_shape=(jax.ShapeDtypeStruct((b, n, d), F32),
                   jax.ShapeDtypeStruct((b, n, d), F32),
                   jax.ShapeDtypeStruct((b, n, N_EXPERTS), F32)),
        grid=(b, n // tm),
        in_specs=[
            pl.BlockSpec((1, tm, d), lambda bi, i: (bi, i, 0)),
            full((1, d)), row(), row(), row(),
            full((d, N_BRANCH * d)), full((1, N_BRANCH * d)),
            pl.BlockSpec((1, tm, POOL_WIDTH + SGU_WIDTH), lambda bi, i: (bi, i, 0)),
            pl.BlockSpec((1, MLA_WIDTH, tm), lambda bi, i: (bi, 0, i)),
            full((POOL_WIDTH, d)), full((SGU_WIDTH, d)), full((MLA_WIDTH, d)), full((d, d)),
            full((1, d)), row(), row(),
            full((d, V7X_LANES)),
        ],
        out_specs=(pl.BlockSpec((1, tm, d), lambda bi, i: (bi, i, 0)),
                   pl.BlockSpec((1, tm, d), lambda bi, i: (bi, i, 0)),
                   pl.BlockSpec((1, tm, N_EXPERTS), lambda bi, i: (bi, i, 0))),
        compiler_params=_cparams(("parallel", "parallel")),
        name="merge",
    )(x, g1n, sh1, sc1, gt1, wg, bg, yps, yat, wbp, wbs, wbm, wout, g2n, sh2, sc2, rw)


def _route_kernel(lg_ref, rb_ref, cls_ref, w_ref):
    scores = jax.nn.sigmoid(lg_ref[...])
    sel = scores + rb_ref[...]
    best = None
    for g in range(N_GROUPS):
        rows = [sel[g * EXPERTS_PER_GROUP + j:g * EXPERTS_PER_GROUP + j + 1] for j in range(EXPERTS_PER_GROUP)]
        srow = [scores[g * EXPERTS_PER_GROUP + j:g * EXPERTS_PER_GROUP + j + 1] for j in range(EXPERTS_PER_GROUP)]
        v1, i1, s1 = rows[0], jnp.zeros_like(rows[0], jnp.int32), srow[0]
        for j in range(1, EXPERTS_PER_GROUP):
            gt = rows[j] > v1
            v1 = jnp.where(gt, rows[j], v1)
            i1 = jnp.where(gt, j, i1)
            s1 = jnp.where(gt, srow[j], s1)
        v2 = jnp.full_like(v1, -jnp.inf)
        i2 = jnp.zeros_like(i1)
        s2 = jnp.zeros_like(s1)
        for j in range(EXPERTS_PER_GROUP):
            gt = (i1 != j) & (rows[j] > v2)
            v2 = jnp.where(gt, rows[j], v2)
            i2 = jnp.where(gt, j, i2)
            s2 = jnp.where(gt, srow[j], s2)
        gs = v1 + v2
        if best is None:
            best = (gs, jnp.full_like(i1, g), i1, i2, s1, s2)
        else:
            gt = gs > best[0]
            cand = (gs, jnp.full_like(i1, g), i1, i2, s1, s2)
            best = tuple(jnp.where(gt, c, b) for c, b in zip(cand, best))
    _, grp, i1, i2, s1, s2 = best
    lo = jnp.minimum(i1, i2)
    hi = jnp.maximum(i1, i2)
    base = jnp.where(lo == 0, 0, jnp.where(lo == 1, 3, 5))
    cls_ref[...] = grp * PAIRS_PER_GROUP + base + (hi - lo - 1)
    tot = s1 + s2
    first_is_lo = i1 < i2
    w_ref[0:1] = jnp.where(first_is_lo, s1, s2) / tot
    w_ref[1:2] = jnp.where(first_is_lo, s2, s1) / tot


def _route(logits_t, rb_t, tn):
    _, t = logits_t.shape
    return pl.pallas_call(
        _route_kernel,
        out_shape=(jax.ShapeDtypeStruct((1, t), jnp.int32), jax.ShapeDtypeStruct((2, t), F32)),
        grid=(t // tn,),
        in_specs=[pl.BlockSpec((N_EXPERTS, tn), lambda i: (0, i)),
                  pl.BlockSpec((N_EXPERTS, tn), lambda i: (0, 0))],
        out_specs=(pl.BlockSpec((1, tn), lambda i: (0, i)), pl.BlockSpec((2, tn), lambda i: (0, i))),
        compiler_params=_cparams(("parallel",)),
        name="route",
    )(logits_t, rb_t)


GATHER_UNROLL = 8


def _start_row_gather(src_hbm, idx_ref, base, dst, sem, rows):
    def body(r, c):
        pltpu.make_async_copy(src_hbm.at[pl.ds(idx_ref[base + r], 1)], dst.at[pl.ds(r, 1)], sem).start()
        return c

    lax.fori_loop(0, rows, body, 0, unroll=GATHER_UNROLL)


def _wait_row_gather(src_hbm, dst, sem, rows):
    pltpu.make_async_copy(src_hbm.at[pl.ds(0, rows)], dst, sem).wait()


def _moe_kernel(elo_ref, ehi_ref, nused_ref, idx_ref, h2_hbm, wab_ref, wgl_ref, wul_ref, wdl_ref,
                wgh_ref, wuh_ref, wdh_ref, y_ref, xbuf, sem, *, tmo):
    i = pl.program_id(0)
    n_used = nused_ref[0]

    @pl.when((i == 0) & (n_used > 0))
    def _():
        _start_row_gather(h2_hbm, idx_ref, 0, xbuf.at[0], sem.at[0], tmo)

    @pl.when(i + 1 < n_used)
    def _():
        nxt = (i + 1) % 2
        _start_row_gather(h2_hbm, idx_ref, (i + 1) * tmo, xbuf.at[nxt], sem.at[nxt], tmo)

    @pl.when(i < n_used)
    def _():
        slot = i % 2
        _wait_row_gather(h2_hbm, xbuf.at[slot], sem.at[slot], tmo)
        xs = xbuf[slot].astype(BF16)
        wab = wab_ref[...]

        def ffn(wg_ref, wu_ref, col):
            gate = jnp.dot(xs, wg_ref[0], preferred_element_type=F32)
            up = jnp.dot(xs, wu_ref[0], preferred_element_type=F32)
            return ((gate * jax.nn.sigmoid(gate)) * up * wab[:, col:col + 1]).astype(BF16)

        y = jnp.dot(ffn(wgl_ref, wul_ref, 0), wdl_ref[0], preferred_element_type=F32)
        y = y + jnp.dot(ffn(wgh_ref, wuh_ref, 1), wdh_ref[0], preferred_element_type=F32)
        y_ref[...] = y

    @pl.when(i >= n_used)
    def _():
        y_ref[...] = jnp.zeros_like(y_ref)


def _moe(tile_lo, tile_hi, n_used, src_idx, h2, wab, wg, wu, wd, tmo):
    t, d = h2.shape
    n_tiles = tile_lo.shape[0]
    de = wg.shape[2]
    lo3 = lambda i, elo, ehi, nu, idx: (elo[i], 0, 0)
    hi3 = lambda i, elo, ehi, nu, idx: (ehi[i], 0, 0)
    grid_spec = pltpu.PrefetchScalarGridSpec(
        num_scalar_prefetch=4,
        grid=(n_tiles,),
        in_specs=[
            pl.BlockSpec(memory_space=pl.ANY),
            pl.BlockSpec((tmo, 2), lambda i, elo, ehi, nu, idx: (i, 0)),
            pl.BlockSpec((1, d, de), lo3), pl.BlockSpec((1, d, de), lo3), pl.BlockSpec((1, de, d), lo3),
            pl.BlockSpec((1, d, de), hi3), pl.BlockSpec((1, d, de), hi3), pl.BlockSpec((1, de, d), hi3),
        ],
        out_specs=pl.BlockSpec((tmo, d), lambda i, elo, ehi, nu, idx: (i, 0)),
        scratch_shapes=[pltpu.VMEM((2, tmo, d), F32), pltpu.SemaphoreType.DMA((2,))],
    )
    return pl.pallas_call(
        functools.partial(_moe_kernel, tmo=tmo),
        out_shape=jax.ShapeDtypeStruct((n_tiles * tmo, d), F32),
        grid_spec=grid_spec,
        compiler_params=_cparams(("arbitrary",)),
        name="moe",
    )(tile_lo, tile_hi, n_used, src_idx, h2, wab, wg, wu, wd, wg, wu, wd)


def _combine_kernel(pos_ref, x_ref, g_ref, y_hbm, o_ref, ybuf, sem, *, tmc):
    i = pl.program_id(0)
    last = pl.num_programs(0) - 1

    @pl.when(i == 0)
    def _():
        _start_row_gather(y_hbm, pos_ref, 0, ybuf.at[0], sem.at[0], tmc)

    @pl.when(i < last)
    def _():
        nxt = (i + 1) % 2
        _start_row_gather(y_hbm, pos_ref, (i + 1) * tmc, ybuf.at[nxt], sem.at[nxt], tmc)

    slot = i % 2
    _wait_row_gather(y_hbm, ybuf.at[slot], sem.at[slot], tmc)
    o_ref[...] = x_ref[...] + g_ref[0] * ybuf[slot]


def _combine(pos, x2d, gate, y_sorted, n, tmc):
    t, d = x2d.shape
    per_batch = n // tmc
    grid_spec = pltpu.PrefetchScalarGridSpec(
        num_scalar_prefetch=1,
        grid=(t // tmc,),
        in_specs=[
            pl.BlockSpec((tmc, d), lambda i, pos: (i, 0)),
            pl.BlockSpec((1, 1, d), lambda i, pos: (i // per_batch, 0, 0)),
            pl.BlockSpec(memory_space=pl.ANY),
        ],
        out_specs=pl.BlockSpec((tmc, d), lambda i, pos: (i, 0)),
        scratch_shapes=[pltpu.VMEM((2, tmc, d), F32), pltpu.SemaphoreType.DMA((2,))],
    )
    return pl.pallas_call(
        functools.partial(_combine_kernel, tmc=tmc),
        out_shape=jax.ShapeDtypeStruct((t, d), F32),
        grid_spec=grid_spec,
        compiler_params=_cparams(("arbitrary",)),
        name="combine",
    )(pos, x2d, gate, y_sorted)


def _plan(cls, tmo):
    t = cls.shape[0]
    n_tiles = t // tmo + N_CLASSES
    order = jnp.argsort(cls, stable=True).astype(jnp.int32)
    counts = jnp.sum(cls[None, :] == jnp.arange(N_CLASSES, dtype=jnp.int32)[:, None], axis=1).astype(jnp.int32)
    tiles = (counts + tmo - 1) // tmo
    tile_end = jnp.cumsum(tiles)
    tile_start = tile_end - tiles
    tok_start = jnp.cumsum(counts) - counts
    n_used = tile_end[-1]
    tile_id = jnp.arange(n_tiles, dtype=jnp.int32)
    tile_cls = jnp.sum(tile_id[:, None] >= tile_end[None, :], axis=1).astype(jnp.int32)
    last_cls = jnp.max(jnp.where(counts > 0, jnp.arange(N_CLASSES, dtype=jnp.int32), 0))
    tile_cls = jnp.where(tile_id < n_used, tile_cls, last_cls)
    grp = tile_cls // PAIRS_PER_GROUP
    pair = tile_cls % PAIRS_PER_GROUP
    lo = jnp.where(pair < 3, 0, jnp.where(pair < 5, 1, 2))
    hi = jnp.where(pair < 3, pair + 1, jnp.where(pair < 5, pair - 1, 3))
    tile_lo = (grp * EXPERTS_PER_GROUP + lo).astype(jnp.int32)
    tile_hi = (grp * EXPERTS_PER_GROUP + hi).astype(jnp.int32)
    row = jnp.arange(n_tiles * tmo, dtype=jnp.int32)
    row_cls = jnp.repeat(tile_cls, tmo)
    rank = row - jnp.repeat(tile_start[tile_cls] * tmo, tmo)
    valid = (rank < counts[row_cls]) & (jnp.repeat(tile_id, tmo) < n_used)
    src = order[jnp.clip(tok_start[row_cls] + jnp.where(valid, rank, 0), 0, t - 1)]
    rank_tok = jnp.zeros((t,), jnp.int32).at[order].set(jnp.arange(t, dtype=jnp.int32)) - tok_start[cls]
    pos = tile_start[cls] * tmo + rank_tok
    return tile_lo, tile_hi, n_used.reshape(1).astype(jnp.int32), src.astype(jnp.int32), pos.astype(jnp.int32)


def _tiles(n):
    big = 512 if n % 512 == 0 else 256
    return dict(inproj=big, poolsgu=big, attn=big, merge=256)


def _project(x, lw, mod, tl):
    return _inproj(x, lw["norm1_g"], mod[0], mod[1], lw["w_puv"], lw["w_qkt"], tl["inproj"])


def _mixer_and_moe(x, puv, yat, lw, mod, rw, rb, tl):
    b, n, d = x.shape
    sh1, sc1, gt1, sh2, sc2, gt2 = mod
    yps = _poolsgu(puv, lw["pool_bd"], lw["pool_scale"], lw["sgu_norm_g"], lw["sgu_ws"], lw["sgu_bias"],
                   tl["poolsgu"])
    xmid, h2, logits = _merge(x, lw["norm1_g"], sh1, sc1, gt1, lw["w_gate"], lw["b_gate"], yps, yat,
                              lw["w_br_pool"], lw["w_br_sgu"], lw["w_br_mla"], lw["w_out"],
                              lw["norm2_g"], sh2, sc2, rw, tl["merge"])
    t = b * n
    tn = next(w for w in (2048, 1024, 512, 256) if t % w == 0)
    cls, w2 = _route(logits.reshape(t, N_EXPERTS).T, jnp.broadcast_to(rb[:, None], (N_EXPERTS, tn)), tn)
    tile_lo, tile_hi, n_used, src, pos = _plan(cls[0], MOE_TILE)
    wab = w2.T[src]
    y_sorted = _moe(tile_lo, tile_hi, n_used, src, h2.reshape(t, d), wab,
                    lw["w_e_gate"], lw["w_e_up"], lw["w_e_down"], MOE_TILE)
    out = _combine(pos, xmid.reshape(t, d), gt2, y_sorted, n, MOE_TILE)
    return out.reshape(b, n, d)


def _rope_tables_t(n):
    rows = n // GRID_W
    row = jnp.repeat(jnp.arange(rows, dtype=F32), GRID_W)
    col = jnp.tile(jnp.arange(GRID_W, dtype=F32), rows)
    inv = ROPE_BASE ** (-(jnp.arange(ROPE_PAIRS, dtype=F32) * 2.0 / ROPE_AXIS))
    ang_r = inv[:, None] * row[None, :]
    ang_c = inv[:, None] * col[None, :]
    ang = jnp.concatenate([ang_r, ang_r, ang_c, ang_c], axis=0)
    return jnp.cos(ang), jnp.sin(ang)


def kernel(x, c, ctx, c_ctx, w_mod, b_mod, norm1_g, norm2_g, w_in, pool_w, pool_scale, sgu_norm_g, sgu_ws, sgu_b, qa_norm_g, w_uq, kva_norm_g, w_ukv, q_norm_g, k_norm_g, w_br_pool, w_br_sgu, w_br_mla, b_gate, w_out, router_w, router_b, w_e_gate, w_e_up, w_e_down):
    bsz, n_lat, d = x.shape
    n_ctx = ctx.shape[1]
    depth = w_mod.shape[0]
    assert bsz + 1 <= V7X_SUBLANES

    cond8 = jnp.zeros((V7X_SUBLANES, d), F32).at[:bsz].set(c).at[bsz].set(c_ctx)
    mods = _adaln(cond8, w_mod, b_mod)
    cos_l, sin_l = _rope_tables_t(n_lat)
    cos_t = jnp.concatenate([cos_l, jnp.ones((QK_ROPE, n_ctx), F32)], axis=1)
    sin_t = jnp.concatenate([sin_l, jnp.zeros((QK_ROPE, n_ctx), F32)], axis=1)
    nl, nc = n_lat // KEY_TILE, n_ctx // KEY_TILE
    bro = lambda g: jnp.broadcast_to(g[:, None], (g.shape[0], KEY_TILE))
    rw_hi = router_w.astype(BF16)
    rw_lo = (router_w - rw_hi.astype(F32)).astype(BF16)
    rw = jnp.concatenate([rw_hi, rw_lo, jnp.zeros((d, V7X_LANES - 2 * N_EXPERTS), BF16)], axis=1)
    tl_lat, tl_ctx = _tiles(n_lat), _tiles(n_ctx)

    x_lat, x_ctx = x, ctx
    for l in range(depth):
        last = l == depth - 1
        wl = w_in[l]
        lw = dict(
            norm1_g=norm1_g[l][None], norm2_g=norm2_g[l][None],
            w_puv=wl[:, :PUV_WIDTH].astype(BF16),
            w_qkt=wl[:, OFF_QA:OFF_GATE].T.astype(BF16),
            w_gate=wl[:, OFF_GATE:].astype(BF16), b_gate=b_gate[l][None],
            pool_bd=jax.scipy.linalg.block_diag(*[pool_w[l, g] for g in range(len(POOL_WINDOWS))]).astype(BF16),
            pool_scale=pool_scale[l][None], sgu_norm_g=sgu_norm_g[l][None],
            sgu_ws=sgu_ws[l].astype(BF16), sgu_bias=jnp.repeat(sgu_b[l].T, SGU_GROUP, axis=1),
            w_uq_t=w_uq[l].T.astype(BF16), w_ukv_t=w_ukv[l].T.astype(BF16),
            w_br_pool=w_br_pool[l].astype(BF16), w_br_sgu=w_br_sgu[l].astype(BF16),
            w_br_mla=w_br_mla[l].astype(BF16), w_out=w_out[l].astype(BF16),
            w_e_gate=w_e_gate[l].astype(BF16), w_e_up=w_e_up[l].astype(BF16), w_e_down=w_e_down[l].astype(BF16),
        )
        m = mods[l]
        mod_lat = [m[:bsz, i * d:(i + 1) * d][:, None, :] for i in range(6)]
        mod_ctx = [jnp.broadcast_to(m[bsz:bsz + 1, i * d:(i + 1) * d][:, None, :], (bsz, 1, d)) for i in range(6)]

        puv_c, qkt_c = _project(x_ctx, lw, mod_ctx, tl_ctx)
        puv, qkt = _project(x_lat, lw, mod_lat, tl_lat)
        qt, k, vt = _qkv(qkt, qkt_c, bro(qa_norm_g[l]), bro(kva_norm_g[l]), bro(q_norm_g[l]), bro(k_norm_g[l]),
                         cos_t, sin_t, lw["w_uq_t"], lw["w_ukv_t"])
        yat = _attention(qt, k, vt, 0, n_lat // tl_lat["attn"], tl_lat["attn"], 0, nl + nc)
        x_lat = _mixer_and_moe(x_lat, puv, yat, lw, mod_lat, rw, router_b, tl_lat)
        if not last:
            yat_c = _attention(qt, k, vt, nl, nc, KEY_TILE, nl, nc)
            x_ctx = _mixer_and_moe(x_ctx, puv_c, yat_c, lw, mod_ctx, rw, router_b, tl_ctx)
    return x_lat
```

```python
import functools
import math

import jax
import jax.numpy as jnp
from jax import lax
from jax.experimental import pallas as pl
from jax.experimental.pallas import tpu as pltpu

F32 = jnp.float32
BF16 = jnp.bfloat16

GRID_W = 64
POOL_WINDOWS = (2, 4, 8, 16)
POOL_WIDTH = 256
POOL_GROUP = POOL_WIDTH // len(POOL_WINDOWS)
SGU_WIDTH = 256
SGU_GROUPS = 4
SGU_GROUP = SGU_WIDTH // SGU_GROUPS
CHUNK = 128
HEADS = 8
QK_NOPE = 64
QK_ROPE = 32
V_HEAD = 64
Q_LORA = 384
KV_LORA = 256
QK_HEAD = QK_NOPE + QK_ROPE
MLA_WIDTH = HEADS * V_HEAD
SM_SCALE = QK_HEAD ** -0.5
ROPE_BASE = 10000.0
ROPE_AXIS = QK_ROPE // 2
ROPE_PAIRS = ROPE_AXIS // 2
N_BRANCH = 3
OFF_U = POOL_WIDTH
OFF_V = OFF_U + SGU_WIDTH
OFF_QA = OFF_V + SGU_WIDTH
OFF_KVA = OFF_QA + Q_LORA
OFF_KR = OFF_KVA + KV_LORA
OFF_GATE = OFF_KR + QK_ROPE
N_EXPERTS = 16
N_GROUPS = 4
EXPERTS_PER_GROUP = N_EXPERTS // N_GROUPS
PAIRS_PER_GROUP = EXPERTS_PER_GROUP * (EXPERTS_PER_GROUP - 1) // 2
N_CLASSES = N_GROUPS * PAIRS_PER_GROUP
EPS = 1e-6

PUV_WIDTH = OFF_QA
QKT_ROWS = OFF_GATE - OFF_QA
HEAD_PAD = 128
QK_EXP2_SCALE = SM_SCALE * math.log2(math.e)

V7X_LANES = 128
V7X_SUBLANES = 8
V7X_VMEM_LIMIT = 52 * 1024 * 1024
KEY_TILE = 256
MOE_TILE = 256
ROW_TILE = (V7X_SUBLANES, V7X_LANES)


def _cparams(sem, vmem=V7X_VMEM_LIMIT):
    return pltpu.CompilerParams(dimension_semantics=sem, vmem_limit_bytes=vmem)


def _norm_mod(x, g, shift, scale):
    ms = jnp.mean(x * x, axis=-1, keepdims=True)
    y = x * lax.rsqrt(ms + EPS) * g
    return y * (1.0 + scale) + shift


def _rms_rows(x, g):
    ms = jnp.mean(x * x, axis=0, keepdims=True)
    return x * lax.rsqrt(ms + EPS) * g


def _adaln_kernel(c_ref, w_ref, b_ref, o_ref):
    c = c_ref[...]
    s = (c * jax.nn.sigmoid(c)).astype(BF16)
    o_ref[0] = jnp.dot(s, w_ref[0].astype(BF16), preferred_element_type=F32) + b_ref[0]


def _adaln(cond8, w_mod, b_mod):
    depth, d, n6 = w_mod.shape
    tn = 1536
    return pl.pallas_call(
        _adaln_kernel,
        out_shape=jax.ShapeDtypeStruct((depth, V7X_SUBLANES, n6), F32),
        grid=(depth, n6 // tn),
        in_specs=[
            pl.BlockSpec((V7X_SUBLANES, d), lambda l, j: (0, 0)),
            pl.BlockSpec((1, d, tn), lambda l, j: (l, 0, j)),
            pl.BlockSpec((1, 1, tn), lambda l, j: (l, 0, j)),
        ],
        out_specs=pl.BlockSpec((1, V7X_SUBLANES, tn), lambda l, j: (l, 0, j)),
        compiler_params=_cparams(("parallel", "parallel")),
        name="adaln",
    )(cond8, w_mod, b_mod.reshape(depth, 1, n6))


def _inproj_kernel(x_ref, g_ref, sh_ref, sc_ref, wn_ref, wt_ref, puv_ref, qkt_ref):
    h = _norm_mod(x_ref[0], g_ref[...], sh_ref[0], sc_ref[0]).astype(BF16)
    puv_ref[0] = jnp.dot(h, wn_ref[...], preferred_element_type=F32)
    qkt_ref[0] = lax.dot_general(wt_ref[...], h, (((1,), (1,)), ((), ())),
                                 preferred_element_type=F32)


def _inproj(x, g, shift, scale, w_nat, w_t, tm):
    b, n, d = x.shape
    return pl.pallas_call(
        _inproj_kernel,
        out_shape=(jax.ShapeDtypeStruct((b, n, PUV_WIDTH), F32),
                   jax.ShapeDtypeStruct((b, QKT_ROWS, n), F32)),
        grid=(b, n // tm),
        in_specs=[
            pl.BlockSpec((1, tm, d), lambda bi, i: (bi, i, 0)),
            pl.BlockSpec((1, d), lambda bi, i: (0, 0)),
            pl.BlockSpec((1, 1, d), lambda bi, i: (bi, 0, 0)),
            pl.BlockSpec((1, 1, d), lambda bi, i: (bi, 0, 0)),
            pl.BlockSpec((d, PUV_WIDTH), lambda bi, i: (0, 0)),
            pl.BlockSpec((QKT_ROWS, d), lambda bi, i: (0, 0)),
        ],
        out_specs=(pl.BlockSpec((1, tm, PUV_WIDTH), lambda bi, i: (bi, i, 0)),
                   pl.BlockSpec((1, QKT_ROWS, tm), lambda bi, i: (bi, 0, i))),
        compiler_params=_cparams(("parallel", "parallel")),
        name="inproj",
    )(x, g, shift, scale, w_nat, w_t)


POOL_HALO = 8


def _poolsgu_kernel(puv_ref, prev_ref, next_ref, poolw_ref, pscale_ref, sgug_ref, ws_ref,
                    bias_ref, o_ref, *, n, tc):
    i = pl.program_id(1)
    last = pl.num_programs(1) - 1
    puv = puv_ref[0]
    p = puv[:, :POOL_WIDTH]
    prev = jnp.where(i > 0, prev_ref[0], 0.0)
    nxt = jnp.where(i < last, next_ref[0], 0.0)
    e = jnp.concatenate([prev, p, nxt], axis=0)
    rows = tc + 2 * POOL_HALO
    a2 = e + pltpu.roll(e, 1, 0)
    a4 = a2 + pltpu.roll(a2, 2, 0)
    a8 = a4 + pltpu.roll(a4, 4, 0)
    a16 = a8 + pltpu.roll(a8, 8, 0)
    w2 = a2[POOL_HALO:POOL_HALO + tc]
    w4 = pltpu.roll(a4, rows - 1, 0)[POOL_HALO:POOL_HALO + tc]
    w8 = pltpu.roll(a8, rows - 3, 0)[POOL_HALO:POOL_HALO + tc]
    w16 = pltpu.roll(a16, rows - 7, 0)[POOL_HALO:POOL_HALO + tc]
    lane = lax.broadcasted_iota(jnp.int32, (tc, POOL_WIDTH), 1)
    t = i * tc + lax.broadcasted_iota(jnp.int32, (tc, POOL_WIDTH), 0)
    grp = lane // POOL_GROUP
    half = jnp.where(grp == 0, 1, jnp.where(grp == 1, 2, jnp.where(grp == 2, 4, 8)))
    cnt = (jnp.minimum(t + half, n) - jnp.maximum(t - half, 0)).astype(F32)
    wsum = jnp.where(grp == 0, w2, jnp.where(grp == 1, w4, jnp.where(grp == 2, w8, w16)))
    dlt = (wsum / cnt - p).astype(BF16)
    y_pool = jnp.dot(dlt, poolw_ref[...], preferred_element_type=F32) * pscale_ref[...]
    o_ref[0, :, 0:POOL_WIDTH] = y_pool.astype(o_ref.dtype)

    u = jax.nn.gelu(puv[:, OFF_U:OFF_V])
    gv = jax.nn.gelu(puv[:, OFF_V:OFF_QA])
    vn = gv * lax.rsqrt(jnp.mean(gv * gv, axis=-1, keepdims=True) + EPS) * sgug_ref[...]
    clane = lax.broadcasted_iota(jnp.int32, (CHUNK, SGU_WIDTH), 1) // SGU_GROUP
    for c in range(tc // CHUNK):
        vc = vn[c * CHUNK:(c + 1) * CHUNK]
        mixed = bias_ref[...]
        for gi in range(SGU_GROUPS):
            vm = jnp.where(clane == gi, vc, 0.0).astype(BF16)
            mixed = mixed + jnp.dot(ws_ref[gi], vm, preferred_element_type=F32)
        y = u[c * CHUNK:(c + 1) * CHUNK] * mixed
        o_ref[0, c * CHUNK:(c + 1) * CHUNK, POOL_WIDTH:POOL_WIDTH + SGU_WIDTH] = y.astype(o_ref.dtype)


def _poolsgu(puv, poolw, pscale, sgug, ws, bias, tc):
    b, n, _ = puv.shape
    nh = n // POOL_HALO
    per = tc // POOL_HALO
    kern = functools.partial(_poolsgu_kernel, n=n, tc=tc)
    return pl.pallas_call(
        kern,
        out_shape=jax.ShapeDtypeStruct((b, n, POOL_WIDTH + SGU_WIDTH), BF16),
        grid=(b, n // tc),
        in_specs=[
            pl.BlockSpec((1, tc, PUV_WIDTH), lambda bi, i: (bi, i, 0)),
            pl.BlockSpec((1, POOL_HALO, POOL_WIDTH),
                         lambda bi, i: (bi, jnp.maximum(i * per - 1, 0), 0)),
            pl.BlockSpec((1, POOL_HALO, POOL_WIDTH),
                         lambda bi, i: (bi, jnp.minimum((i + 1) * per, nh - 1), 0)),
            pl.BlockSpec((POOL_WIDTH, POOL_WIDTH), lambda bi, i: (0, 0)),
            pl.BlockSpec((1, POOL_WIDTH), lambda bi, i: (0, 0)),
            pl.BlockSpec((1, SGU_WIDTH), lambda bi, i: (0, 0)),
            pl.BlockSpec((SGU_GROUPS, CHUNK, CHUNK), lambda bi, i: (0, 0, 0)),
            pl.BlockSpec((CHUNK, SGU_WIDTH), lambda bi, i: (0, 0)),
        ],
        out_specs=pl.BlockSpec((1, tc, POOL_WIDTH + SGU_WIDTH), lambda bi, i: (bi, i, 0)),
        compiler_params=_cparams(("parallel", "parallel")),
        name="poolsgu",
    )(puv, puv, puv, poolw, pscale, sgug, ws, bias)


V_ROWS = 80


def _qkv_kernel(lat_ref, ctx_ref, gqa_ref, gkva_ref, gq_ref, gk_ref, cos_ref, sin_ref, wq_ref, wkv_ref,
                qt_ref, k_ref, vt_ref, *, n_lat_tiles):
    tm = KEY_TILE
    blk = jnp.where(pl.program_id(1) < n_lat_tiles, lat_ref[0], ctx_ref[0])
    qa_n = _rms_rows(blk[0:Q_LORA], gqa_ref[...]).astype(BF16)
    kva_n = _rms_rows(blk[Q_LORA:Q_LORA + KV_LORA], gkva_ref[...]).astype(BF16)
    q = jnp.dot(wq_ref[...], qa_n, preferred_element_type=F32)
    kv = jnp.dot(wkv_ref[...], kva_n, preferred_element_type=F32)
    gq = gq_ref[...]
    gk = gk_ref[...]

    def rope(x):
        r = ROPE_PAIRS
        rot = jnp.concatenate([-x[r:2 * r], x[0:r], -x[3 * r:4 * r], x[2 * r:3 * r]], axis=0)
        return x * cos_ref[...] + rot * sin_ref[...]

    k_rope = rope(_rms_rows(blk[Q_LORA + KV_LORA:QKT_ROWS], gk[QK_NOPE:QK_HEAD]))
    pad = jnp.zeros((HEAD_PAD - QK_HEAD, tm), F32)
    ones_rows = jnp.where(lax.broadcasted_iota(jnp.int32, (V_ROWS - V_HEAD, tm), 0) == 0, 1.0, 0.0)
    kv_rows = QK_NOPE + V_HEAD
    for h in range(HEADS):
        qh = q[h * QK_HEAD:(h + 1) * QK_HEAD]
        q_nope = _rms_rows(qh[0:QK_NOPE], gq[0:QK_NOPE])
        q_rope = rope(_rms_rows(qh[QK_NOPE:QK_HEAD], gq[QK_NOPE:QK_HEAD]))
        qcat = jnp.concatenate([q_nope, q_rope, pad], axis=0) * QK_EXP2_SCALE
        qt_ref[0, h] = qcat.astype(qt_ref.dtype)
        k_nope = _rms_rows(kv[h * kv_rows:h * kv_rows + QK_NOPE], gk[0:QK_NOPE])
        kcat = jnp.concatenate([k_nope, k_rope, pad], axis=0)
        k_ref[0, h, 0] = kcat.T.astype(k_ref.dtype)
        v = jnp.concatenate([kv[h * kv_rows + QK_NOPE:(h + 1) * kv_rows], ones_rows], axis=0)
        vt_ref[0, h, 0] = v.astype(vt_ref.dtype)


def _qkv(qkt_lat, qkt_ctx, gqa, gkva, gq, gk, cos_t, sin_t, wq_t, wkv_t):
    b, _, n_lat = qkt_lat.shape
    n_ctx = qkt_ctx.shape[2]
    tm = KEY_TILE
    nl, nc = n_lat // tm, n_ctx // tm
    nt = nl + nc
    kern = functools.partial(_qkv_kernel, n_lat_tiles=nl)
    full = lambda shape: pl.BlockSpec(shape, lambda bi, i: (0,) * len(shape))
    return pl.pallas_call(
        kern,
        out_shape=(jax.ShapeDtypeStruct((b, HEADS, HEAD_PAD, nt * tm), BF16),
                   jax.ShapeDtypeStruct((b, HEADS, nt, tm, HEAD_PAD), BF16),
                   jax.ShapeDtypeStruct((b, HEADS, nt, V_ROWS, tm), BF16)),
        grid=(b, nt),
        in_specs=[
            pl.BlockSpec((1, QKT_ROWS, tm), lambda bi, i: (bi, 0, jnp.minimum(i, nl - 1))),
            pl.BlockSpec((1, QKT_ROWS, tm), lambda bi, i: (bi, 0, jnp.maximum(i - nl, 0))),
            full((Q_LORA, tm)), full((KV_LORA, tm)), full((QK_HEAD, tm)), full((QK_HEAD, tm)),
            pl.BlockSpec((QK_ROPE, tm), lambda bi, i: (0, i)),
            pl.BlockSpec((QK_ROPE, tm), lambda bi, i: (0, i)),
            full((HEADS * QK_HEAD, Q_LORA)), full((HEADS * (QK_NOPE + V_HEAD), KV_LORA)),
        ],
        out_specs=(pl.BlockSpec((1, HEADS, HEAD_PAD, tm), lambda bi, i: (bi, 0, 0, i)),
                   pl.BlockSpec((1, HEADS, 1, tm, HEAD_PAD), lambda bi, i: (bi, 0, i, 0, 0)),
                   pl.BlockSpec((1, HEADS, 1, V_ROWS, tm), lambda bi, i: (bi, 0, i, 0, 0))),
        compiler_params=_cparams(("parallel", "parallel")),
        name="qkv",
    )(qkt_lat, qkt_ctx, gqa, gkva, gq, gk, cos_t, sin_t, wq_t, wkv_t)


def _attn_kernel(qt_ref, k_ref, vt_ref, o_ref, m_ref, acc_ref, s_ref, *, g_n, s_n, per):
    m_ref[...] = jnp.full(m_ref.shape, -jnp.inf, F32)
    acc_ref[...] = jnp.zeros(acc_ref.shape, F32)

    def qk(g, s, j, slot):
        s_ref[slot, g * s_n + s] = jnp.dot(k_ref[0, g, s * per + j], qt_ref[0, g],
                                            preferred_element_type=F32)

    def soft_pv(g, s, j, slot):
        ci = g * s_n + s
        sc = s_ref[slot, ci]
        m = m_ref[ci]
        m_new = jnp.maximum(m, jnp.max(sc, axis=0, keepdims=True))
        alpha = jnp.exp2(m - m_new)
        p = jnp.exp2(sc - m_new).astype(BF16)
        acc_ref[ci] = alpha * acc_ref[ci] + jnp.dot(vt_ref[0, g, s * per + j], p,
                                                     preferred_element_type=F32)
        m_ref[ci] = m_new

    for g in range(g_n):
        for s in range(s_n):
            qk(g, s, 0, 0)

    def body(jj, c):
        for half in range(2):
            j = 2 * jj + half
            for g in range(g_n):
                for s in range(s_n):
                    qk(g, s, j + 1, 1 - half)
                    soft_pv(g, s, j, half)
        return c

    if per > 1:
        lax.fori_loop(0, (per - 1) // 2, body, 0)
    for g in range(g_n):
        for s in range(s_n):
            soft_pv(g, s, per - 1, (per - 1) % 2)
    for g in range(g_n):
        m = m_ref[g * s_n]
        for s in range(1, s_n):
            m = jnp.maximum(m, m_ref[g * s_n + s])
        tot = None
        for s in range(s_n):
            part = acc_ref[g * s_n + s] * jnp.exp2(m_ref[g * s_n + s] - m)
            tot = part if tot is None else tot + part
        o_ref[0, g * V_HEAD:(g + 1) * V_HEAD] = (tot[0:V_HEAD] / tot[V_HEAD:V_HEAD + 1]).astype(o_ref.dtype)


ATTN_HEADS_PER_STEP = 2


def _attention(qt, k, vt, q0, nq, tq, key0, nkeys):
    b = qt.shape[0]
    g_n = ATTN_HEADS_PER_STEP
    s_n = next(s for s in (3, 1) if nkeys % s == 0 and (nkeys // s - 1) % 2 == 0)
    assert key0 % nkeys == 0 and HEADS % g_n == 0
    kb = key0 // nkeys
    kern = functools.partial(_attn_kernel, g_n=g_n, s_n=s_n, per=nkeys // s_n)
    return pl.pallas_call(
        kern,
        out_shape=jax.ShapeDtypeStruct((b, MLA_WIDTH, nq * tq), BF16),
        grid=(b, HEADS // g_n, nq),
        in_specs=[
            pl.BlockSpec((1, g_n, HEAD_PAD, tq), lambda bi, hg, i: (bi, hg, 0, q0 + i)),
            pl.BlockSpec((1, g_n, nkeys, KEY_TILE, HEAD_PAD), lambda bi, hg, i: (bi, hg, kb, 0, 0)),
            pl.BlockSpec((1, g_n, nkeys, V_ROWS, KEY_TILE), lambda bi, hg, i: (bi, hg, kb, 0, 0)),
        ],
        out_specs=pl.BlockSpec((1, g_n * V_HEAD, tq), lambda bi, hg, i: (bi, hg, i)),
        scratch_shapes=[pltpu.VMEM((g_n * s_n, 1, tq), F32),
                        pltpu.VMEM((g_n * s_n, V_ROWS, tq), F32),
                        pltpu.VMEM((2, g_n * s_n, KEY_TILE, tq), F32)],
        compiler_params=_cparams(("parallel", "parallel", "parallel")),
        name="attention",
    )(qt, k, vt)


def _router_logits(h2, rw):
    hi = h2.astype(BF16)
    lo = (h2 - hi.astype(F32)).astype(BF16)
    r_hi = jnp.dot(hi, rw, preferred_element_type=F32)
    r_lo = jnp.dot(lo, rw, preferred_element_type=F32)
    lg = r_hi + pltpu.roll(r_hi, V7X_LANES - N_EXPERTS, 1) + r_lo
    lane = lax.broadcasted_iota(jnp.int32, lg.shape, 1)
    return jnp.where(lane < N_EXPERTS, lg, 0.0)


def _merge_kernel(x_ref, g1n_ref, sh1_ref, sc1_ref, gt1_ref, wg_ref, bg_ref, yps_ref, yat_ref,
                  wbp_ref, wbs_ref, wbm_ref, wout_ref, g2n_ref, sh2_ref, sc2_ref, rw_ref,
                  xmid_ref, h2_ref, lg_ref):
    x = x_ref[0]
    d = x.shape[1]
    h = _norm_mod(x, g1n_ref[...], sh1_ref[0], sc1_ref[0]).astype(BF16)
    yps = yps_ref[0]
    branches = (
        jnp.dot(yps[:, 0:POOL_WIDTH], wbp_ref[...], preferred_element_type=F32),
        jnp.dot(yps[:, POOL_WIDTH:POOL_WIDTH + SGU_WIDTH], wbs_ref[...], preferred_element_type=F32),
        lax.dot_general(yat_ref[0], wbm_ref[...], (((0,), (0,)), ((), ())),
                        preferred_element_type=F32),
    )
    merged = None
    for i in range(N_BRANCH):
        gl = jnp.dot(h, wg_ref[:, i * d:(i + 1) * d], preferred_element_type=F32) + bg_ref[:, i * d:(i + 1) * d]
        term = jax.nn.sigmoid(gl) * branches[i]
        merged = term if merged is None else merged + term
    out = jnp.dot(merged.astype(BF16), wout_ref[...], preferred_element_type=F32)
    xm = x + gt1_ref[0] * out
    xmid_ref[0] = xm
    h2 = _norm_mod(xm, g2n_ref[...], sh2_ref[0], sc2_ref[0])
    h2_ref[0] = h2.reshape((h2.shape[0],) + ROW_TILE)
    lg_ref[0] = _router_logits(h2, rw_ref[...])


def _merge(x, g1n, sh1, sc1, gt1, wg, bg, yps, yat, wbp, wbs, wbm, wout, g2n, sh2, sc2, rw, tm):
    b, n, d = x.shape
    row = lambda: pl.BlockSpec((1, 1, d), lambda bi, i: (bi, 0, 0))
    full = lambda shape: pl.BlockSpec(shape, lambda bi, i: (0,) * len(shape))
    return pl.pallas_call(
        _merge_kernel,
        out_shape=(jax.ShapeDtypeStruct((b, n, d), F32),
                   jax.ShapeDtypeStruct((b, n) + ROW_TILE, F32),
                   jax.ShapeDtypeStruct((b, n, V7X_LANES), F32)),
        grid=(b, n // tm),
        in_specs=[
            pl.BlockSpec((1, tm, d), lambda bi, i: (bi, i, 0)),
            full((1, d)), row(), row(), row(),
            full((d, N_BRANCH * d)), full((1, N_BRANCH * d)),
            pl.BlockSpec((1, tm, POOL_WIDTH + SGU_WIDTH), lambda bi, i: (bi, i, 0)),
            pl.BlockSpec((1, MLA_WIDTH, tm), lambda bi, i: (bi, 0, i)),
            full((POOL_WIDTH, d)), full((SGU_WIDTH, d)), full((MLA_WIDTH, d)), full((d, d)),
            full((1, d)), row(), row(),
            full((d, V7X_LANES)),
        ],
        out_specs=(pl.BlockSpec((1, tm, d), lambda bi, i: (bi, i, 0)),
                   pl.BlockSpec((1, tm) + ROW_TILE, lambda bi, i: (bi, i, 0, 0)),
                   pl.BlockSpec((1, tm, V7X_LANES), lambda bi, i: (bi, i, 0))),
        compiler_params=_cparams(("parallel", "parallel")),
        name="merge",
    )(x, g1n, sh1, sc1, gt1, wg, bg, yps, yat, wbp, wbs, wbm, wout, g2n, sh2, sc2, rw)


CLS_ROWS = 32


def _route_kernel(ext_ref, rb_ref, tri_ref, init_ref, cls_ref, rank_ref, cnt_ref, carry_ref):
    tn = ext_ref.shape[0]

    @pl.when(pl.program_id(0) == 0)
    def _():
        carry_ref[...] = jnp.broadcast_to(init_ref[:, 0:1], carry_ref.shape)

    scores = jax.nn.sigmoid(ext_ref[...].T[0:N_EXPERTS])
    sel = scores + rb_ref[...]
    best = None
    for g in range(N_GROUPS):
        rows = [sel[g * EXPERTS_PER_GROUP + j:g * EXPERTS_PER_GROUP + j + 1] for j in range(EXPERTS_PER_GROUP)]
        v1, i1 = rows[0], jnp.zeros_like(rows[0], jnp.int32)
        for j in range(1, EXPERTS_PER_GROUP):
            gt = rows[j] > v1
            v1 = jnp.where(gt, rows[j], v1)
            i1 = jnp.where(gt, j, i1)
        v2 = jnp.full_like(v1, -jnp.inf)
        i2 = jnp.zeros_like(i1)
        for j in range(EXPERTS_PER_GROUP):
            gt = (i1 != j) & (rows[j] > v2)
            v2 = jnp.where(gt, rows[j], v2)
            i2 = jnp.where(gt, j, i2)
        gs = v1 + v2
        if best is None:
            best = (gs, jnp.full_like(i1, g), i1, i2)
        else:
            gt = gs > best[0]
            cand = (gs, jnp.full_like(i1, g), i1, i2)
            best = tuple(jnp.where(gt, c, b) for c, b in zip(cand, best))
    _, grp, i1, i2 = best
    lo = jnp.minimum(i1, i2)
    hi = jnp.maximum(i1, i2)
    base = jnp.where(lo == 0, 0, jnp.where(lo == 1, 3, 5))
    cls = grp * PAIRS_PER_GROUP + base + (hi - lo - 1)
    cls_ref[...] = cls
    onehot = jnp.where(lax.broadcasted_iota(jnp.int32, (CLS_ROWS, tn), 0) == cls, 1.0, 0.0)
    cum = jnp.dot(onehot.astype(BF16), tri_ref[...], preferred_element_type=F32)
    carry = carry_ref[...]
    rank_ref[...] = (jnp.sum(onehot * (carry + cum), axis=0, keepdims=True) - 1.0).astype(jnp.int32)
    carry = carry + jnp.broadcast_to(cum[:, tn - 1:tn], carry.shape)
    carry_ref[...] = carry
    cnt_ref[...] = carry[:, 0:V7X_LANES]


def _route(logits, rb, init_counts):
    t = logits.shape[0]
    tn = next(w for w in (512, 256) if t % w == 0)
    tri = (jnp.arange(tn)[:, None] <= jnp.arange(tn)[None, :]).astype(BF16)
    return pl.pallas_call(
        _route_kernel,
        out_shape=(jax.ShapeDtypeStruct((1, t), jnp.int32), jax.ShapeDtypeStruct((1, t), jnp.int32),
                   jax.ShapeDtypeStruct((CLS_ROWS, V7X_LANES), F32)),
        grid=(t // tn,),
        in_specs=[pl.BlockSpec((tn, V7X_LANES), lambda i: (i, 0)),
                  pl.BlockSpec((N_EXPERTS, tn), lambda i: (0, 0)),
                  pl.BlockSpec((tn, tn), lambda i: (0, 0)),
                  pl.BlockSpec((CLS_ROWS, V7X_LANES), lambda i: (0, 0))],
        out_specs=(pl.BlockSpec((1, tn), lambda i: (0, i)), pl.BlockSpec((1, tn), lambda i: (0, i)),
                   pl.BlockSpec((CLS_ROWS, V7X_LANES), lambda i: (0, 0))),
        scratch_shapes=[pltpu.VMEM((CLS_ROWS, tn), F32)],
        compiler_params=_cparams(("arbitrary",)),
        name="route",
    )(logits, jnp.broadcast_to(rb[:, None], (N_EXPERTS, tn)), tri, init_counts)


def _plan(cls, rank, counts, tmo):
    t = cls.shape[0]
    n_tiles = t // tmo + N_CLASSES
    class_ids = jnp.arange(N_CLASSES, dtype=jnp.int32)
    tiles = (counts + tmo - 1) // tmo
    tile_end = jnp.cumsum(tiles)
    tile_start = tile_end - tiles
    n_used = tile_end[-1]
    pos = rank + jnp.sum(jnp.where(cls[None, :] == class_ids[:, None], (tile_start * tmo)[:, None], 0), axis=0)
    tile_id = jnp.arange(n_tiles, dtype=jnp.int32)
    tile_cls = jnp.sum(tile_id[:, None] >= tile_end[None, :], axis=1).astype(jnp.int32)
    last_cls = jnp.max(jnp.where(counts > 0, class_ids, 0))
    tile_cls = jnp.where(tile_id < n_used, tile_cls, last_cls)
    grp = tile_cls // PAIRS_PER_GROUP
    pair = tile_cls % PAIRS_PER_GROUP
    lo = jnp.where(pair < 3, 0, jnp.where(pair < 5, 1, 2))
    hi = jnp.where(pair < 3, pair + 1, jnp.where(pair < 5, pair - 1, 3))
    tile_lo = (grp * EXPERTS_PER_GROUP + lo).astype(jnp.int32)
    tile_hi = (grp * EXPERTS_PER_GROUP + hi).astype(jnp.int32)
    pad_start = (tile_start * tmo + counts).astype(jnp.int32)
    pad_len = (tiles * tmo - counts).astype(jnp.int32)
    return (pos.astype(jnp.int32), tile_lo, tile_hi, n_used.reshape(1).astype(jnp.int32), pad_start, pad_len)


GATHER_UNROLL = 8
PAD_BITS = 8


def _dispatch_kernel(pos_ref, pstart_ref, plen_ref, nused_ref, *refs, src_steps, rs):
    n_src = len(src_steps)
    srcs, out, zbuf, sem = refs[:n_src], refs[n_src], refs[n_src + 1], refs[n_src + 2]
    i = pl.program_id(0)
    last = pl.num_programs(0) - 1

    def pad_copies(act):
        for c in range(N_CLASSES):
            ln, st = plen_ref[c], pstart_ref[c]
            for k in range(PAD_BITS):
                size = 1 << k

                @pl.when((ln & size) != 0)
                def _(size=size, ln=ln, st=st):
                    off = st + (ln & (size - 1))
                    act(pltpu.make_async_copy(zbuf.at[pl.ds(0, size)], out.at[pl.ds(off, size)], sem.at[2]))
        half = zbuf.shape[0]
        n_tiles = out.shape[0] // (2 * half)
        for j in range(n_tiles - N_CLASSES, n_tiles):
            @pl.when(j >= nused_ref[0])
            def _(j=j):
                for part in range(2):
                    act(pltpu.make_async_copy(zbuf, out.at[pl.ds((2 * j + part) * half, half)], sem.at[2]))

    @pl.when(i == 0)
    def _():
        zbuf[...] = jnp.zeros_like(zbuf)
        pad_copies(lambda cp: cp.start())

    first = 0
    for src, steps in zip(srcs, src_steps):
        @pl.when((i >= first) & (i < first + steps))
        def _(src=src, first=first):
            def body(r, c):
                pltpu.make_async_copy(src.at[pl.ds((i - first) * rs + r, 1)],
                                      out.at[pl.ds(pos_ref[i * rs + r], 1)], sem.at[i % 2]).start()
                return c

            lax.fori_loop(0, rs, body, 0, unroll=GATHER_UNROLL)
        first += steps

    def wait_rows(slot):
        pltpu.make_async_copy(srcs[0].at[pl.ds(0, rs)], out.at[pl.ds(0, rs)], sem.at[slot]).wait()

    @pl.when(i > 0)
    def _():
        wait_rows((i - 1) % 2)

    @pl.when(i == last)
    def _():
        wait_rows(i % 2)
        pad_copies(lambda cp: cp.wait())


def _dispatch(pos, pad_start, pad_len, n_used, sources, n_rows):
    rs = next(w for w in (512, 256) if all(s.shape[0] % w == 0 for s in sources))
    src_steps = tuple(s.shape[0] // rs for s in sources)
    grid_spec = pltpu.PrefetchScalarGridSpec(
        num_scalar_prefetch=4,
        grid=(sum(src_steps),),
        in_specs=[pl.BlockSpec(memory_space=pl.ANY)] * len(sources),
        out_specs=pl.BlockSpec(memory_space=pl.ANY),
        scratch_shapes=[pltpu.VMEM((1 << (PAD_BITS - 1),) + ROW_TILE, F32), pltpu.SemaphoreType.DMA((3,))],
    )
    return pl.pallas_call(
        functools.partial(_dispatch_kernel, src_steps=src_steps, rs=rs),
        out_shape=jax.ShapeDtypeStruct((n_rows,) + ROW_TILE, F32),
        grid_spec=grid_spec,
        compiler_params=_cparams(("arbitrary",)),
        name="dispatch",
    )(pos, pad_start, pad_len, n_used, *sources)


def _moe_kernel(elo_ref, ehi_ref, nused_ref, xs_ref, rw_ref, wgl_ref, wul_ref, wdl_ref,
                wgh_ref, wuh_ref, wdh_ref, y_ref):
    i = pl.program_id(0)
    tmo = xs_ref.shape[0]

    @pl.when(i < nused_ref[0])
    def _():
        x = xs_ref[...].reshape(tmo, -1)
        xs = x.astype(BF16)
        scores = jax.nn.sigmoid(jnp.dot(xs, rw_ref[...], preferred_element_type=F32))
        lane = lax.broadcasted_iota(jnp.int32, scores.shape, 1)
        s_lo = jnp.sum(jnp.where(lane == elo_ref[i], scores, 0.0), axis=1, keepdims=True)
        s_hi = jnp.sum(jnp.where(lane == ehi_ref[i], scores, 0.0), axis=1, keepdims=True)
        tot = s_lo + s_hi

        def ffn(wg_ref, wu_ref, w_row):
            gate = jnp.dot(xs, wg_ref[0], preferred_element_type=F32)
            up = jnp.dot(xs, wu_ref[0], preferred_element_type=F32)
            return ((gate * jax.nn.sigmoid(gate)) * up * w_row).astype(BF16)

        y = jnp.dot(ffn(wgl_ref, wul_ref, s_lo / tot), wdl_ref[0], preferred_element_type=F32)
        y = y + jnp.dot(ffn(wgh_ref, wuh_ref, s_hi / tot), wdh_ref[0], preferred_element_type=F32)
        y_ref[...] = y.reshape(y_ref.shape)

    @pl.when(i >= nused_ref[0])
    def _():
        y_ref[...] = jnp.zeros_like(y_ref)


def _moe(tile_lo, tile_hi, n_used, x_sorted, rw, wg, wu, wd, tmo):
    n_tiles = x_sorted.shape[0] // tmo
    d, de = wg.shape[1], wg.shape[2]
    lo3 = lambda i, elo, ehi, nu: (elo[i], 0, 0)
    hi3 = lambda i, elo, ehi, nu: (ehi[i], 0, 0)
    grid_spec = pltpu.PrefetchScalarGridSpec(
        num_scalar_prefetch=3,
        grid=(n_tiles,),
        in_specs=[
            pl.BlockSpec((tmo,) + ROW_TILE, lambda i, elo, ehi, nu: (jnp.minimum(i, nu[0] - 1), 0, 0)),
            pl.BlockSpec((d, V7X_LANES), lambda i, elo, ehi, nu: (0, 0)),
            pl.BlockSpec((1, d, de), lo3), pl.BlockSpec((1, d, de), lo3), pl.BlockSpec((1, de, d), lo3),
            pl.BlockSpec((1, d, de), hi3), pl.BlockSpec((1, d, de), hi3), pl.BlockSpec((1, de, d), hi3),
        ],
        out_specs=pl.BlockSpec((tmo,) + ROW_TILE, lambda i, elo, ehi, nu: (i, 0, 0)),
    )
    return pl.pallas_call(
        _moe_kernel,
        out_shape=jax.ShapeDtypeStruct((n_tiles * tmo,) + ROW_TILE, F32),
        grid_spec=grid_spec,
        compiler_params=_cparams(("arbitrary",)),
        name="moe",
    )(tile_lo, tile_hi, n_used, x_sorted, rw, wg, wu, wd, wg, wu, wd)


def _start_row_gather(src_hbm, idx_ref, base, dst, sem, rows):
    def body(r, c):
        pltpu.make_async_copy(src_hbm.at[pl.ds(idx_ref[base + r], 1)], dst.at[pl.ds(r, 1)], sem).start()
        return c

    lax.fori_loop(0, rows, body, 0, unroll=GATHER_UNROLL)


def _wait_row_gather(src_hbm, dst, sem, rows):
    pltpu.make_async_copy(src_hbm.at[pl.ds(0, rows)], dst, sem).wait()


def _combine_kernel(pos_ref, x_ref, g_ref, y_hbm, o_ref, ybuf, sem, *, tmc):
    i = pl.program_id(0)
    last = pl.num_programs(0) - 1

    @pl.when(i == 0)
    def _():
        _start_row_gather(y_hbm, pos_ref, 0, ybuf.at[0], sem.at[0], tmc)

    @pl.when(i < last)
    def _():
        nxt = (i + 1) % 2
        _start_row_gather(y_hbm, pos_ref, (i + 1) * tmc, ybuf.at[nxt], sem.at[nxt], tmc)

    slot = i % 2
    _wait_row_gather(y_hbm, ybuf.at[slot], sem.at[slot], tmc)
    o_ref[...] = x_ref[...] + g_ref[0] * ybuf[slot].reshape(o_ref.shape)


def _combine(pos, x2d, gate, y_sorted, n, tmc):
    t, d = x2d.shape
    per_batch = n // tmc
    grid_spec = pltpu.PrefetchScalarGridSpec(
        num_scalar_prefetch=1,
        grid=(t // tmc,),
        in_specs=[
            pl.BlockSpec((tmc, d), lambda i, pos: (i, 0)),
            pl.BlockSpec((1, 1, d), lambda i, pos: (i // per_batch, 0, 0)),
            pl.BlockSpec(memory_space=pl.ANY),
        ],
        out_specs=pl.BlockSpec((tmc, d), lambda i, pos: (i, 0)),
        scratch_shapes=[pltpu.VMEM((2, tmc) + ROW_TILE, F32), pltpu.SemaphoreType.DMA((2,))],
    )
    return pl.pallas_call(
        functools.partial(_combine_kernel, tmc=tmc),
        out_shape=jax.ShapeDtypeStruct((t, d), F32),
        grid_spec=grid_spec,
        compiler_params=_cparams(("arbitrary",)),
        name="combine",
    )(pos, x2d, gate, y_sorted)


def _tiles(n):
    big = 512 if n % 512 == 0 else 256
    return dict(inproj=big, poolsgu=big, attn=big, merge=256)


def _project(x, lw, mod, tl):
    return _inproj(x, lw["norm1_g"], mod[0], mod[1], lw["w_puv"], lw["w_qkt"], tl["inproj"])


def _mixer(x, puv, yat, lw, mod, rw, tl):
    b, n, d = x.shape
    sh1, sc1, gt1, sh2, sc2, _ = mod
    yps = _poolsgu(puv, lw["pool_bd"], lw["pool_scale"], lw["sgu_norm_g"], lw["sgu_ws"], lw["sgu_bias"],
                   tl["poolsgu"])
    xmid, h2r, lg = _merge(x, lw["norm1_g"], sh1, sc1, gt1, lw["w_gate"], lw["b_gate"], yps, yat,
                           lw["w_br_pool"], lw["w_br_sgu"], lw["w_br_mla"], lw["w_out"],
                           lw["norm2_g"], sh2, sc2, rw, tl["merge"])
    return xmid.reshape(b * n, d), h2r.reshape((b * n,) + ROW_TILE), lg.reshape(b * n, V7X_LANES)


def _moe_block(streams, lw, rw, rb):
    counts = jnp.zeros((CLS_ROWS, V7X_LANES), F32)
    cls_l, rank_l = [], []
    for st in streams:
        cls, rank, counts = _route(st[2], rb, counts)
        cls_l.append(cls[0])
        rank_l.append(rank[0])
    t_all = sum(c.shape[0] for c in cls_l)
    pos, tile_lo, tile_hi, n_used, pad_start, pad_len = _plan(
        jnp.concatenate(cls_l), jnp.concatenate(rank_l), counts[:N_CLASSES, 0].astype(jnp.int32), MOE_TILE)
    n_rows = (t_all // MOE_TILE + N_CLASSES) * MOE_TILE
    x_sorted = _dispatch(pos, pad_start, pad_len, n_used, [st[1] for st in streams], n_rows)
    y_sorted = _moe(tile_lo, tile_hi, n_used, x_sorted, rw, lw["w_e_gate"], lw["w_e_up"], lw["w_e_down"], MOE_TILE)
    outs, first = [], 0
    for xmid, _, _, gate, n in streams:
        t = xmid.shape[0]
        outs.append(_combine(pos[first:first + t], xmid, gate, y_sorted, n, MOE_TILE))
        first += t
    return outs


def _rope_tables_t(n):
    rows = n // GRID_W
    row = jnp.repeat(jnp.arange(rows, dtype=F32), GRID_W)
    col = jnp.tile(jnp.arange(GRID_W, dtype=F32), rows)
    inv = ROPE_BASE ** (-(jnp.arange(ROPE_PAIRS, dtype=F32) * 2.0 / ROPE_AXIS))
    ang_r = inv[:, None] * row[None, :]
    ang_c = inv[:, None] * col[None, :]
    ang = jnp.concatenate([ang_r, ang_r, ang_c, ang_c], axis=0)
    return jnp.cos(ang), jnp.sin(ang)


def kernel(x, c, ctx, c_ctx, w_mod, b_mod, norm1_g, norm2_g, w_in, pool_w, pool_scale, sgu_norm_g, sgu_ws, sgu_b, qa_norm_g, w_uq, kva_norm_g, w_ukv, q_norm_g, k_norm_g, w_br_pool, w_br_sgu, w_br_mla, b_gate, w_out, router_w, router_b, w_e_gate, w_e_up, w_e_down):
    bsz, n_lat, d = x.shape
    n_ctx = ctx.shape[1]
    depth = w_mod.shape[0]
    assert bsz + 1 <= V7X_SUBLANES

    cond8 = jnp.zeros((V7X_SUBLANES, d), F32).at[:bsz].set(c).at[bsz].set(c_ctx)
    mods = _adaln(cond8, w_mod, b_mod)
    cos_l, sin_l = _rope_tables_t(n_lat)
    cos_t = jnp.concatenate([cos_l, jnp.ones((QK_ROPE, n_ctx), F32)], axis=1)
    sin_t = jnp.concatenate([sin_l, jnp.zeros((QK_ROPE, n_ctx), F32)], axis=1)
    nl, nc = n_lat // KEY_TILE, n_ctx // KEY_TILE
    bro = lambda g: jnp.broadcast_to(g[:, None], (g.shape[0], KEY_TILE))
    rw_hi = router_w.astype(BF16)
    rw_lo = (router_w - rw_hi.astype(F32)).astype(BF16)
    rw = jnp.concatenate([rw_hi, rw_lo, jnp.zeros((d, V7X_LANES - 2 * N_EXPERTS), BF16)], axis=1)
    tl_lat, tl_ctx = _tiles(n_lat), _tiles(n_ctx)

    x_lat, x_ctx = x, ctx
    for l in range(depth):
        last = l == depth - 1
        wl = w_in[l]
        lw = dict(
            norm1_g=norm1_g[l][None], norm2_g=norm2_g[l][None],
            w_puv=wl[:, :PUV_WIDTH].astype(BF16),
            w_qkt=wl[:, OFF_QA:OFF_GATE].T.astype(BF16),
            w_gate=wl[:, OFF_GATE:].astype(BF16), b_gate=b_gate[l][None],
            pool_bd=jax.scipy.linalg.block_diag(*[pool_w[l, g] for g in range(len(POOL_WINDOWS))]).astype(BF16),
            pool_scale=pool_scale[l][None], sgu_norm_g=sgu_norm_g[l][None],
            sgu_ws=sgu_ws[l].astype(BF16), sgu_bias=jnp.repeat(sgu_b[l].T, SGU_GROUP, axis=1),
            w_uq_t=w_uq[l].T.astype(BF16), w_ukv_t=w_ukv[l].T.astype(BF16),
            w_br_pool=w_br_pool[l].astype(BF16), w_br_sgu=w_br_sgu[l].astype(BF16),
            w_br_mla=w_br_mla[l].astype(BF16), w_out=w_out[l].astype(BF16),
            w_e_gate=w_e_gate[l].astype(BF16), w_e_up=w_e_up[l].astype(BF16), w_e_down=w_e_down[l].astype(BF16),
        )
        m = mods[l]
        mod_lat = [m[:bsz, i * d:(i + 1) * d][:, None, :] for i in range(6)]
        mod_ctx = [jnp.broadcast_to(m[bsz:bsz + 1, i * d:(i + 1) * d][:, None, :], (bsz, 1, d)) for i in range(6)]

        puv_c, qkt_c = _project(x_ctx, lw, mod_ctx, tl_ctx)
        puv, qkt = _project(x_lat, lw, mod_lat, tl_lat)
        qt, k, vt = _qkv(qkt, qkt_c, bro(qa_norm_g[l]), bro(kva_norm_g[l]), bro(q_norm_g[l]), bro(k_norm_g[l]),
                         cos_t, sin_t, lw["w_uq_t"], lw["w_ukv_t"])
        yat = _attention(qt, k, vt, 0, n_lat // tl_lat["attn"], tl_lat["attn"], 0, nl + nc)
        streams = [_mixer(x_lat, puv, yat, lw, mod_lat, rw, tl_lat) + (mod_lat[5], n_lat)]
        if not last:
            yat_c = _attention(qt, k, vt, nl, nc, KEY_TILE, nl, nc)
            streams.append(_mixer(x_ctx, puv_c, yat_c, lw, mod_ctx, rw, tl_ctx) + (mod_ctx[5], n_ctx))
        outs = _moe_block(streams, lw, rw, router_b)
        x_lat = outs[0].reshape(bsz, n_lat, d)
        if not last:
            x_ctx = outs[1].reshape(bsz, n_ctx, d)
    return x_lat
```

```python
import functools
import math

import jax
import jax.numpy as jnp
from jax import lax
from jax.experimental import pallas as pl
from jax.experimental.pallas import tpu as pltpu

F32 = jnp.float32
BF16 = jnp.bfloat16

GRID_W = 64
POOL_WINDOWS = (2, 4, 8, 16)
POOL_WIDTH = 256
POOL_GROUP = POOL_WIDTH // len(POOL_WINDOWS)
SGU_WIDTH = 256
SGU_GROUPS = 4
SGU_GROUP = SGU_WIDTH // SGU_GROUPS
CHUNK = 128
HEADS = 8
QK_NOPE = 64
QK_ROPE = 32
V_HEAD = 64
Q_LORA = 384
KV_LORA = 256
QK_HEAD = QK_NOPE + QK_ROPE
MLA_WIDTH = HEADS * V_HEAD
SM_SCALE = QK_HEAD ** -0.5
ROPE_BASE = 10000.0
ROPE_AXIS = QK_ROPE // 2
ROPE_PAIRS = ROPE_AXIS // 2
N_BRANCH = 3
OFF_U = POOL_WIDTH
OFF_V = OFF_U + SGU_WIDTH
OFF_QA = OFF_V + SGU_WIDTH
OFF_KVA = OFF_QA + Q_LORA
OFF_KR = OFF_KVA + KV_LORA
OFF_GATE = OFF_KR + QK_ROPE
N_EXPERTS = 16
N_GROUPS = 4
EXPERTS_PER_GROUP = N_EXPERTS // N_GROUPS
PAIRS_PER_GROUP = EXPERTS_PER_GROUP * (EXPERTS_PER_GROUP - 1) // 2
N_CLASSES = N_GROUPS * PAIRS_PER_GROUP
EPS = 1e-6

PUV_WIDTH = OFF_QA
QKT_ROWS = OFF_GATE - OFF_QA
HEAD_PAD = 128
QK_EXP2_SCALE = SM_SCALE * math.log2(math.e)

V7X_LANES = 128
V7X_SUBLANES = 8
V7X_VMEM_LIMIT = 52 * 1024 * 1024
KEY_TILE = 256
MOE_TILE = 256
ROW_TILE = (V7X_SUBLANES, V7X_LANES)


def _cparams(sem, vmem=V7X_VMEM_LIMIT):
    return pltpu.CompilerParams(dimension_semantics=sem, vmem_limit_bytes=vmem)


def _norm_mod(x, g, shift, scale):
    ms = jnp.mean(x * x, axis=-1, keepdims=True)
    y = x * lax.rsqrt(ms + EPS) * g
    return y * (1.0 + scale) + shift


def _rms_rows(x, g):
    ms = jnp.mean(x * x, axis=0, keepdims=True)
    return x * lax.rsqrt(ms + EPS) * g


def _adaln_kernel(c_ref, w_ref, b_ref, o_ref):
    c = c_ref[...]
    s = (c * jax.nn.sigmoid(c)).astype(BF16)
    o_ref[0] = jnp.dot(s, w_ref[0].astype(BF16), preferred_element_type=F32) + b_ref[0]


def _adaln(cond8, w_mod, b_mod):
    depth, d, n6 = w_mod.shape
    tn = 1536
    return pl.pallas_call(
        _adaln_kernel,
        out_shape=jax.ShapeDtypeStruct((depth, V7X_SUBLANES, n6), F32),
        grid=(depth, n6 // tn),
        in_specs=[
            pl.BlockSpec((V7X_SUBLANES, d), lambda l, j: (0, 0)),
            pl.BlockSpec((1, d, tn), lambda l, j: (l, 0, j)),
            pl.BlockSpec((1, 1, tn), lambda l, j: (l, 0, j)),
        ],
        out_specs=pl.BlockSpec((1, V7X_SUBLANES, tn), lambda l, j: (l, 0, j)),
        compiler_params=_cparams(("parallel", "parallel")),
        name="adaln",
    )(cond8, w_mod, b_mod.reshape(depth, 1, n6))


def _inproj_kernel(x_ref, g_ref, sh_ref, sc_ref, wn_ref, wt_ref, puv_ref, qkt_ref):
    h = _norm_mod(x_ref[0], g_ref[...], sh_ref[0], sc_ref[0]).astype(BF16)
    puv_ref[0] = jnp.dot(h, wn_ref[...], preferred_element_type=F32)
    qkt_ref[0] = lax.dot_general(wt_ref[...], h, (((1,), (1,)), ((), ())),
                                 preferred_element_type=F32)


def _inproj(x, g, shift, scale, w_nat, w_t, tm):
    b, n, d = x.shape
    return pl.pallas_call(
        _inproj_kernel,
        out_shape=(jax.ShapeDtypeStruct((b, n, PUV_WIDTH), F32),
                   jax.ShapeDtypeStruct((b, QKT_ROWS, n), F32)),
        grid=(b, n // tm),
        in_specs=[
            pl.BlockSpec((1, tm, d), lambda bi, i: (bi, i, 0)),
            pl.BlockSpec((1, d), lambda bi, i: (0, 0)),
            pl.BlockSpec((1, 1, d), lambda bi, i: (bi, 0, 0)),
            pl.BlockSpec((1, 1, d), lambda bi, i: (bi, 0, 0)),
            pl.BlockSpec((d, PUV_WIDTH), lambda bi, i: (0, 0)),
            pl.BlockSpec((QKT_ROWS, d), lambda bi, i: (0, 0)),
        ],
        out_specs=(pl.BlockSpec((1, tm, PUV_WIDTH), lambda bi, i: (bi, i, 0)),
                   pl.BlockSpec((1, QKT_ROWS, tm), lambda bi, i: (bi, 0, i))),
        compiler_params=_cparams(("parallel", "parallel")),
        name="inproj",
    )(x, g, shift, scale, w_nat, w_t)


POOL_HALO = 8


def _poolsgu_kernel(puv_ref, prev_ref, next_ref, poolw_ref, pscale_ref, sgug_ref, ws_ref,
                    bias_ref, o_ref, *, n, tc):
    i = pl.program_id(1)
    last = pl.num_programs(1) - 1
    puv = puv_ref[0]
    p = puv[:, :POOL_WIDTH]
    prev = jnp.where(i > 0, prev_ref[0], 0.0)
    nxt = jnp.where(i < last, next_ref[0], 0.0)
    e = jnp.concatenate([prev, p, nxt], axis=0)
    rows = tc + 2 * POOL_HALO
    a2 = e + pltpu.roll(e, 1, 0)
    a4 = a2 + pltpu.roll(a2, 2, 0)
    a8 = a4 + pltpu.roll(a4, 4, 0)
    a16 = a8 + pltpu.roll(a8, 8, 0)
    w2 = a2[POOL_HALO:POOL_HALO + tc]
    w4 = pltpu.roll(a4, rows - 1, 0)[POOL_HALO:POOL_HALO + tc]
    w8 = pltpu.roll(a8, rows - 3, 0)[POOL_HALO:POOL_HALO + tc]
    w16 = pltpu.roll(a16, rows - 7, 0)[POOL_HALO:POOL_HALO + tc]
    lane = lax.broadcasted_iota(jnp.int32, (tc, POOL_WIDTH), 1)
    t = i * tc + lax.broadcasted_iota(jnp.int32, (tc, POOL_WIDTH), 0)
    grp = lane // POOL_GROUP
    half = jnp.where(grp == 0, 1, jnp.where(grp == 1, 2, jnp.where(grp == 2, 4, 8)))
    cnt = (jnp.minimum(t + half, n) - jnp.maximum(t - half, 0)).astype(F32)
    wsum = jnp.where(grp == 0, w2, jnp.where(grp == 1, w4, jnp.where(grp == 2, w8, w16)))
    dlt = (wsum / cnt - p).astype(BF16)
    y_pool = jnp.dot(dlt, poolw_ref[...], preferred_element_type=F32) * pscale_ref[...]
    o_ref[0, :, 0:POOL_WIDTH] = y_pool.astype(o_ref.dtype)

    u = jax.nn.gelu(puv[:, OFF_U:OFF_V])
    gv = jax.nn.gelu(puv[:, OFF_V:OFF_QA])
    vn = gv * lax.rsqrt(jnp.mean(gv * gv, axis=-1, keepdims=True) + EPS) * sgug_ref[...]
    clane = lax.broadcasted_iota(jnp.int32, (CHUNK, SGU_WIDTH), 1) // SGU_GROUP
    for c in range(tc // CHUNK):
        vc = vn[c * CHUNK:(c + 1) * CHUNK]
        mixed = bias_ref[...]
        for gi in range(SGU_GROUPS):
            vm = jnp.where(clane == gi, vc, 0.0).astype(BF16)
            mixed = mixed + jnp.dot(ws_ref[gi], vm, preferred_element_type=F32)
        y = u[c * CHUNK:(c + 1) * CHUNK] * mixed
        o_ref[0, c * CHUNK:(c + 1) * CHUNK, POOL_WIDTH:POOL_WIDTH + SGU_WIDTH] = y.astype(o_ref.dtype)


def _poolsgu(puv, poolw, pscale, sgug, ws, bias, tc):
    b, n, _ = puv.shape
    nh = n // POOL_HALO
    per = tc // POOL_HALO
    kern = functools.partial(_poolsgu_kernel, n=n, tc=tc)
    return pl.pallas_call(
        kern,
        out_shape=jax.ShapeDtypeStruct((b, n, POOL_WIDTH + SGU_WIDTH), BF16),
        grid=(b, n // tc),
        in_specs=[
            pl.BlockSpec((1, tc, PUV_WIDTH), lambda bi, i: (bi, i, 0)),
            pl.BlockSpec((1, POOL_HALO, POOL_WIDTH),
                         lambda bi, i: (bi, jnp.maximum(i * per - 1, 0), 0)),
            pl.BlockSpec((1, POOL_HALO, POOL_WIDTH),
                         lambda bi, i: (bi, jnp.minimum((i + 1) * per, nh - 1), 0)),
            pl.BlockSpec((POOL_WIDTH, POOL_WIDTH), lambda bi, i: (0, 0)),
            pl.BlockSpec((1, POOL_WIDTH), lambda bi, i: (0, 0)),
            pl.BlockSpec((1, SGU_WIDTH), lambda bi, i: (0, 0)),
            pl.BlockSpec((SGU_GROUPS, CHUNK, CHUNK), lambda bi, i: (0, 0, 0)),
            pl.BlockSpec((CHUNK, SGU_WIDTH), lambda bi, i: (0, 0)),
        ],
        out_specs=pl.BlockSpec((1, tc, POOL_WIDTH + SGU_WIDTH), lambda bi, i: (bi, i, 0)),
        compiler_params=_cparams(("parallel", "parallel")),
        name="poolsgu",
    )(puv, puv, puv, poolw, pscale, sgug, ws, bias)


V_ROWS = 80


def _qkv_kernel(lat_ref, ctx_ref, gqa_ref, gkva_ref, gq_ref, gk_ref, cos_ref, sin_ref, wq_ref, wkv_ref,
                qt_ref, k_ref, vt_ref, *, n_lat_tiles):
    tm = KEY_TILE
    blk = jnp.where(pl.program_id(1) < n_lat_tiles, lat_ref[0], ctx_ref[0])
    qa_n = _rms_rows(blk[0:Q_LORA], gqa_ref[...]).astype(BF16)
    kva_n = _rms_rows(blk[Q_LORA:Q_LORA + KV_LORA], gkva_ref[...]).astype(BF16)
    q = jnp.dot(wq_ref[...], qa_n, preferred_element_type=F32)
    kv = jnp.dot(wkv_ref[...], kva_n, preferred_element_type=F32)
    gq = gq_ref[...]
    gk = gk_ref[...]

    def rope(x):
        r = ROPE_PAIRS
        rot = jnp.concatenate([-x[r:2 * r], x[0:r], -x[3 * r:4 * r], x[2 * r:3 * r]], axis=0)
        return x * cos_ref[...] + rot * sin_ref[...]

    k_rope = rope(_rms_rows(blk[Q_LORA + KV_LORA:QKT_ROWS], gk[QK_NOPE:QK_HEAD]))
    pad = jnp.zeros((HEAD_PAD - QK_HEAD, tm), F32)
    ones_rows = jnp.where(lax.broadcasted_iota(jnp.int32, (V_ROWS - V_HEAD, tm), 0) == 0, 1.0, 0.0)
    kv_rows = QK_NOPE + V_HEAD
    for h in range(HEADS):
        qh = q[h * QK_HEAD:(h + 1) * QK_HEAD]
        q_nope = _rms_rows(qh[0:QK_NOPE], gq[0:QK_NOPE])
        q_rope = rope(_rms_rows(qh[QK_NOPE:QK_HEAD], gq[QK_NOPE:QK_HEAD]))
        qcat = jnp.concatenate([q_nope, q_rope, pad], axis=0) * QK_EXP2_SCALE
        qt_ref[0, h] = qcat.astype(qt_ref.dtype)
        k_nope = _rms_rows(kv[h * kv_rows:h * kv_rows + QK_NOPE], gk[0:QK_NOPE])
        kcat = jnp.concatenate([k_nope, k_rope, pad], axis=0)
        k_ref[0, h, 0] = kcat.T.astype(k_ref.dtype)
        v = jnp.concatenate([kv[h * kv_rows + QK_NOPE:(h + 1) * kv_rows], ones_rows], axis=0)
        vt_ref[0, h, 0] = v.astype(vt_ref.dtype)


def _qkv(qkt_lat, qkt_ctx, gqa, gkva, gq, gk, cos_t, sin_t, wq_t, wkv_t):
    b, _, n_lat = qkt_lat.shape
    n_ctx = qkt_ctx.shape[2]
    tm = KEY_TILE
    nl, nc = n_lat // tm, n_ctx // tm
    nt = nl + nc
    kern = functools.partial(_qkv_kernel, n_lat_tiles=nl)
    full = lambda shape: pl.BlockSpec(shape, lambda bi, i: (0,) * len(shape))
    return pl.pallas_call(
        kern,
        out_shape=(jax.ShapeDtypeStruct((b, HEADS, HEAD_PAD, nt * tm), BF16),
                   jax.ShapeDtypeStruct((b, HEADS, nt, tm, HEAD_PAD), BF16),
                   jax.ShapeDtypeStruct((b, HEADS, nt, V_ROWS, tm), BF16)),
        grid=(b, nt),
        in_specs=[
            pl.BlockSpec((1, QKT_ROWS, tm), lambda bi, i: (bi, 0, jnp.minimum(i, nl - 1))),
            pl.BlockSpec((1, QKT_ROWS, tm), lambda bi, i: (bi, 0, jnp.maximum(i - nl, 0))),
            full((Q_LORA, tm)), full((KV_LORA, tm)), full((QK_HEAD, tm)), full((QK_HEAD, tm)),
            pl.BlockSpec((QK_ROPE, tm), lambda bi, i: (0, i)),
            pl.BlockSpec((QK_ROPE, tm), lambda bi, i: (0, i)),
            full((HEADS * QK_HEAD, Q_LORA)), full((HEADS * (QK_NOPE + V_HEAD), KV_LORA)),
        ],
        out_specs=(pl.BlockSpec((1, HEADS, HEAD_PAD, tm), lambda bi, i: (bi, 0, 0, i)),
                   pl.BlockSpec((1, HEADS, 1, tm, HEAD_PAD), lambda bi, i: (bi, 0, i, 0, 0)),
                   pl.BlockSpec((1, HEADS, 1, V_ROWS, tm), lambda bi, i: (bi, 0, i, 0, 0))),
        compiler_params=_cparams(("parallel", "parallel")),
        name="qkv",
    )(qkt_lat, qkt_ctx, gqa, gkva, gq, gk, cos_t, sin_t, wq_t, wkv_t)


def _attn_kernel(qt_ref, k_ref, vt_ref, o_ref, m_ref, acc_ref, s_ref, *, g_n, s_n, per):
    m_ref[...] = jnp.full(m_ref.shape, -jnp.inf, F32)
    acc_ref[...] = jnp.zeros(acc_ref.shape, F32)

    def qk(g, s, j, slot):
        s_ref[slot, g * s_n + s] = jnp.dot(k_ref[0, g, s * per + j], qt_ref[0, g],
                                            preferred_element_type=F32)

    def soft_pv(g, s, j, slot):
        ci = g * s_n + s
        sc = s_ref[slot, ci]
        m = m_ref[ci]
        m_new = jnp.maximum(m, jnp.max(sc, axis=0, keepdims=True))
        alpha = jnp.exp2(m - m_new)
        p = jnp.exp2(sc - m_new).astype(BF16)
        acc_ref[ci] = alpha * acc_ref[ci] + jnp.dot(vt_ref[0, g, s * per + j], p,
                                                     preferred_element_type=F32)
        m_ref[ci] = m_new

    for g in range(g_n):
        for s in range(s_n):
            qk(g, s, 0, 0)

    def body(jj, c):
        for half in range(2):
            j = 2 * jj + half
            for g in range(g_n):
                for s in range(s_n):
                    qk(g, s, j + 1, 1 - half)
                    soft_pv(g, s, j, half)
        return c

    if per > 1:
        lax.fori_loop(0, (per - 1) // 2, body, 0)
    for g in range(g_n):
        for s in range(s_n):
            soft_pv(g, s, per - 1, (per - 1) % 2)
    for g in range(g_n):
        m = m_ref[g * s_n]
        for s in range(1, s_n):
            m = jnp.maximum(m, m_ref[g * s_n + s])
        tot = None
        for s in range(s_n):
            part = acc_ref[g * s_n + s] * jnp.exp2(m_ref[g * s_n + s] - m)
            tot = part if tot is None else tot + part
        o_ref[0, g * V_HEAD:(g + 1) * V_HEAD] = (tot[0:V_HEAD] / tot[V_HEAD:V_HEAD + 1]).astype(o_ref.dtype)


ATTN_HEADS_PER_STEP = 2


def _attention(qt, k, vt, q0, nq, tq, key0, nkeys):
    b = qt.shape[0]
    g_n = ATTN_HEADS_PER_STEP
    s_n = next(s for s in (3, 1) if nkeys % s == 0 and (nkeys // s - 1) % 2 == 0)
    assert key0 % nkeys == 0 and HEADS % g_n == 0
    kb = key0 // nkeys
    kern = functools.partial(_attn_kernel, g_n=g_n, s_n=s_n, per=nkeys // s_n)
    return pl.pallas_call(
        kern,
        out_shape=jax.ShapeDtypeStruct((b, MLA_WIDTH, nq * tq), BF16),
        grid=(b, HEADS // g_n, nq),
        in_specs=[
            pl.BlockSpec((1, g_n, HEAD_PAD, tq), lambda bi, hg, i: (bi, hg, 0, q0 + i)),
            pl.BlockSpec((1, g_n, nkeys, KEY_TILE, HEAD_PAD), lambda bi, hg, i: (bi, hg, kb, 0, 0)),
            pl.BlockSpec((1, g_n, nkeys, V_ROWS, KEY_TILE), lambda bi, hg, i: (bi, hg, kb, 0, 0)),
        ],
        out_specs=pl.BlockSpec((1, g_n * V_HEAD, tq), lambda bi, hg, i: (bi, hg, i)),
        scratch_shapes=[pltpu.VMEM((g_n * s_n, 1, tq), F32),
                        pltpu.VMEM((g_n * s_n, V_ROWS, tq), F32),
                        pltpu.VMEM((2, g_n * s_n, KEY_TILE, tq), F32)],
        compiler_params=_cparams(("parallel", "parallel", "parallel")),
        name="attention",
    )(qt, k, vt)


def _router_logits(h2, rw):
    hi = h2.astype(BF16)
    lo = (h2 - hi.astype(F32)).astype(BF16)
    r_hi = jnp.dot(hi, rw, preferred_element_type=F32)
    r_lo = jnp.dot(lo, rw, preferred_element_type=F32)
    lg = r_hi + pltpu.roll(r_hi, V7X_LANES - N_EXPERTS, 1) + r_lo
    lane = lax.broadcasted_iota(jnp.int32, lg.shape, 1)
    return jnp.where(lane < N_EXPERTS, lg, 0.0)


def _merge_kernel(x_ref, g1n_ref, sh1_ref, sc1_ref, gt1_ref, wg_ref, bg_ref, yps_ref, yat_ref,
                  wbp_ref, wbs_ref, wbm_ref, wout_ref, g2n_ref, sh2_ref, sc2_ref, rw_ref,
                  xmid_ref, h2_ref, lg_ref):
    x = x_ref[0]
    d = x.shape[1]
    h = _norm_mod(x, g1n_ref[...], sh1_ref[0], sc1_ref[0]).astype(BF16)
    yps = yps_ref[0]
    branches = (
        jnp.dot(yps[:, 0:POOL_WIDTH], wbp_ref[...], preferred_element_type=F32),
        jnp.dot(yps[:, POOL_WIDTH:POOL_WIDTH + SGU_WIDTH], wbs_ref[...], preferred_element_type=F32),
        lax.dot_general(yat_ref[0], wbm_ref[...], (((0,), (0,)), ((), ())),
                        preferred_element_type=F32),
    )
    merged = None
    for i in range(N_BRANCH):
        gl = jnp.dot(h, wg_ref[:, i * d:(i + 1) * d], preferred_element_type=F32) + bg_ref[:, i * d:(i + 1) * d]
        term = jax.nn.sigmoid(gl) * branches[i]
        merged = term if merged is None else merged + term
    out = jnp.dot(merged.astype(BF16), wout_ref[...], preferred_element_type=F32)
    xm = x + gt1_ref[0] * out
    xmid_ref[0] = xm
    h2 = _norm_mod(xm, g2n_ref[...], sh2_ref[0], sc2_ref[0])
    h2_ref[0] = h2.reshape((h2.shape[0],) + ROW_TILE)
    lg_ref[0] = _router_logits(h2, rw_ref[...])


def _merge(x, g1n, sh1, sc1, gt1, wg, bg, yps, yat, wbp, wbs, wbm, wout, g2n, sh2, sc2, rw, tm):
    b, n, d = x.shape
    row = lambda: pl.BlockSpec((1, 1, d), lambda bi, i: (bi, 0, 0))
    full = lambda shape: pl.BlockSpec(shape, lambda bi, i: (0,) * len(shape))
    return pl.pallas_call(
        _merge_kernel,
        out_shape=(jax.ShapeDtypeStruct((b, n, d), F32),
                   jax.ShapeDtypeStruct((b, n) + ROW_TILE, F32),
                   jax.ShapeDtypeStruct((b, n, V7X_LANES), F32)),
        grid=(b, n // tm),
        in_specs=[
            pl.BlockSpec((1, tm, d), lambda bi, i: (bi, i, 0)),
            full((1, d)), row(), row(), row(),
            full((d, N_BRANCH * d)), full((1, N_BRANCH * d)),
            pl.BlockSpec((1, tm, POOL_WIDTH + SGU_WIDTH), lambda bi, i: (bi, i, 0)),
            pl.BlockSpec((1, MLA_WIDTH, tm), lambda bi, i: (bi, 0, i)),
            full((POOL_WIDTH, d)), full((SGU_WIDTH, d)), full((MLA_WIDTH, d)), full((d, d)),
            full((1, d)), row(), row(),
            full((d, V7X_LANES)),
        ],
        out_specs=(pl.BlockSpec((1, tm, d), lambda bi, i: (bi, i, 0)),
                   pl.BlockSpec((1, tm) + ROW_TILE, lambda bi, i: (bi, i, 0, 0)),
                   pl.BlockSpec((1, tm, V7X_LANES), lambda bi, i: (bi, i, 0))),
        compiler_params=_cparams(("parallel", "parallel")),
        name="merge",
    )(x, g1n, sh1, sc1, gt1, wg, bg, yps, yat, wbp, wbs, wbm, wout, g2n, sh2, sc2, rw)


CLS_ROWS = 32


def _route_kernel(ext_ref, rb_ref, tri_ref, init_ref, cls_ref, rank_ref, cnt_ref, carry_ref):
    tn = ext_ref.shape[0]

    @pl.when(pl.program_id(0) == 0)
    def _():
        carry_ref[...] = jnp.broadcast_to(init_ref[:, 0:1], carry_ref.shape)

    scores = jax.nn.sigmoid(ext_ref[...].T[0:N_EXPERTS])
    sel = scores + rb_ref[...]
    best = None
    for g in range(N_GROUPS):
        rows = [sel[g * EXPERTS_PER_GROUP + j:g * EXPERTS_PER_GROUP + j + 1] for j in range(EXPERTS_PER_GROUP)]
        v1, i1 = rows[0], jnp.zeros_like(rows[0], jnp.int32)
        for j in range(1, EXPERTS_PER_GROUP):
            gt = rows[j] > v1
            v1 = jnp.where(gt, rows[j], v1)
            i1 = jnp.where(gt, j, i1)
        v2 = jnp.full_like(v1, -jnp.inf)
        i2 = jnp.zeros_like(i1)
        for j in range(EXPERTS_PER_GROUP):
            gt = (i1 != j) & (rows[j] > v2)
            v2 = jnp.where(gt, rows[j], v2)
            i2 = jnp.where(gt, j, i2)
        gs = v1 + v2
        if best is None:
            best = (gs, jnp.full_like(i1, g), i1, i2)
        else:
            gt = gs > best[0]
            cand = (gs, jnp.full_like(i1, g), i1, i2)
            best = tuple(jnp.where(gt, c, b) for c, b in zip(cand, best))
    _, grp, i1, i2 = best
    lo = jnp.minimum(i1, i2)
    hi = jnp.maximum(i1, i2)
    base = jnp.where(lo == 0, 0, jnp.where(lo == 1, 3, 5))
    cls = grp * PAIRS_PER_GROUP + base + (hi - lo - 1)
    cls_ref[...] = cls
    onehot = jnp.where(lax.broadcasted_iota(jnp.int32, (CLS_ROWS, tn), 0) == cls, 1.0, 0.0)
    cum = jnp.dot(onehot.astype(BF16), tri_ref[...], preferred_element_type=F32)
    carry = carry_ref[...]
    rank_ref[...] = (jnp.sum(onehot * (carry + cum), axis=0, keepdims=True) - 1.0).astype(jnp.int32)
    carry = carry + jnp.broadcast_to(cum[:, tn - 1:tn], carry.shape)
    carry_ref[...] = carry
    cnt_ref[...] = carry[:, 0:V7X_LANES]


def _route(logits, rb, init_counts):
    t = logits.shape[0]
    tn = next(w for w in (512, 256) if t % w == 0)
    tri = (jnp.arange(tn)[:, None] <= jnp.arange(tn)[None, :]).astype(BF16)
    return pl.pallas_call(
        _route_kernel,
        out_shape=(jax.ShapeDtypeStruct((1, t), jnp.int32), jax.ShapeDtypeStruct((1, t), jnp.int32),
                   jax.ShapeDtypeStruct((CLS_ROWS, V7X_LANES), F32)),
        grid=(t // tn,),
        in_specs=[pl.BlockSpec((tn, V7X_LANES), lambda i: (i, 0)),
                  pl.BlockSpec((N_EXPERTS, tn), lambda i: (0, 0)),
                  pl.BlockSpec((tn, tn), lambda i: (0, 0)),
                  pl.BlockSpec((CLS_ROWS, V7X_LANES), lambda i: (0, 0))],
        out_specs=(pl.BlockSpec((1, tn), lambda i: (0, i)), pl.BlockSpec((1, tn), lambda i: (0, i)),
                   pl.BlockSpec((CLS_ROWS, V7X_LANES), lambda i: (0, 0))),
        scratch_shapes=[pltpu.VMEM((CLS_ROWS, tn), F32)],
        compiler_params=_cparams(("arbitrary",)),
        name="route",
    )(logits, jnp.broadcast_to(rb[:, None], (N_EXPERTS, tn)), tri, init_counts)


def _plan(cls, rank, counts, tmo):
    t = cls.shape[0]
    n_tiles = t // tmo + N_CLASSES
    class_ids = jnp.arange(N_CLASSES, dtype=jnp.int32)
    tiles = (counts + tmo - 1) // tmo
    tile_end = jnp.cumsum(tiles)
    tile_start = tile_end - tiles
    n_used = tile_end[-1]
    pos = rank + jnp.sum(jnp.where(cls[None, :] == class_ids[:, None], (tile_start * tmo)[:, None], 0), axis=0)
    tile_id = jnp.arange(n_tiles, dtype=jnp.int32)
    tile_cls = jnp.sum(tile_id[:, None] >= tile_end[None, :], axis=1).astype(jnp.int32)
    last_cls = jnp.max(jnp.where(counts > 0, class_ids, 0))
    tile_cls = jnp.where(tile_id < n_used, tile_cls, last_cls)
    grp = tile_cls // PAIRS_PER_GROUP
    pair = tile_cls % PAIRS_PER_GROUP
    lo = jnp.where(pair < 3, 0, jnp.where(pair < 5, 1, 2))
    hi = jnp.where(pair < 3, pair + 1, jnp.where(pair < 5, pair - 1, 3))
    tile_lo = (grp * EXPERTS_PER_GROUP + lo).astype(jnp.int32)
    tile_hi = (grp * EXPERTS_PER_GROUP + hi).astype(jnp.int32)
    pad_start = (tile_start * tmo + counts).astype(jnp.int32)
    pad_len = (tiles * tmo - counts).astype(jnp.int32)
    return (pos.astype(jnp.int32), tile_lo, tile_hi, n_used.reshape(1).astype(jnp.int32), pad_start, pad_len)


GATHER_UNROLL = 8
PAD_BITS = 8


def _dispatch_kernel(pos_ref, pstart_ref, plen_ref, nused_ref, *refs, src_steps, rs):
    n_src = len(src_steps)
    srcs, out, zbuf, sem = refs[:n_src], refs[n_src], refs[n_src + 1], refs[n_src + 2]
    i = pl.program_id(0)
    last = pl.num_programs(0) - 1

    def pad_copies(act):
        for c in range(N_CLASSES):
            ln, st = plen_ref[c], pstart_ref[c]
            for k in range(PAD_BITS):
                size = 1 << k

                @pl.when((ln & size) != 0)
                def _(size=size, ln=ln, st=st):
                    off = st + (ln & (size - 1))
                    act(pltpu.make_async_copy(zbuf.at[pl.ds(0, size)], out.at[pl.ds(off, size)], sem.at[1]))
        half = zbuf.shape[0]
        n_tiles = out.shape[0] // (2 * half)
        for j in range(n_tiles - N_CLASSES, n_tiles):
            @pl.when(j >= nused_ref[0])
            def _(j=j):
                for part in range(2):
                    act(pltpu.make_async_copy(zbuf, out.at[pl.ds((2 * j + part) * half, half)], sem.at[1]))

    @pl.when(i == 0)
    def _():
        zbuf[...] = jnp.zeros_like(zbuf)
        pad_copies(lambda cp: cp.start())

    first = 0
    for src, steps in zip(srcs, src_steps):
        @pl.when((i >= first) & (i < first + steps))
        def _(src=src):
            def body(r, c):
                pltpu.make_async_copy(src.at[pl.ds(r, 1)], out.at[pl.ds(pos_ref[i * rs + r], 1)], sem.at[0]).start()
                return c

            lax.fori_loop(0, rs, body, 0, unroll=GATHER_UNROLL)
            pltpu.make_async_copy(src, out.at[pl.ds(0, rs)], sem.at[0]).wait()
        first += steps

    @pl.when(i == last)
    def _():
        pad_copies(lambda cp: cp.wait())


def _dispatch(pos, pad_start, pad_len, n_used, sources, n_rows):
    rs = next(w for w in (512, 256) if all(s.shape[0] % w == 0 for s in sources))
    src_steps = tuple(s.shape[0] // rs for s in sources)
    firsts = [sum(src_steps[:k]) for k in range(len(sources))]

    def block(first, steps):
        return pl.BlockSpec((rs,) + ROW_TILE, lambda i, *_: (jnp.clip(i - first, 0, steps - 1), 0, 0))

    grid_spec = pltpu.PrefetchScalarGridSpec(
        num_scalar_prefetch=4,
        grid=(sum(src_steps),),
        in_specs=[block(f, st) for f, st in zip(firsts, src_steps)],
        out_specs=pl.BlockSpec(memory_space=pl.ANY),
        scratch_shapes=[pltpu.VMEM((1 << (PAD_BITS - 1),) + ROW_TILE, F32), pltpu.SemaphoreType.DMA((2,))],
    )
    return pl.pallas_call(
        functools.partial(_dispatch_kernel, src_steps=src_steps, rs=rs),
        out_shape=jax.ShapeDtypeStruct((n_rows,) + ROW_TILE, F32),
        grid_spec=grid_spec,
        compiler_params=_cparams(("arbitrary",)),
        name="dispatch",
    )(pos, pad_start, pad_len, n_used, *sources)


def _moe_kernel(elo_ref, ehi_ref, nused_ref, xs_ref, rw_ref, wgl_ref, wul_ref, wdl_ref,
                wgh_ref, wuh_ref, wdh_ref, y_ref):
    i = pl.program_id(0)
    tmo = xs_ref.shape[0]

    @pl.when(i < nused_ref[0])
    def _():
        x = xs_ref[...].reshape(tmo, -1)
        xs = x.astype(BF16)
        scores = jax.nn.sigmoid(jnp.dot(xs, rw_ref[...], preferred_element_type=F32))
        lane = lax.broadcasted_iota(jnp.int32, scores.shape, 1)
        s_lo = jnp.sum(jnp.where(lane == elo_ref[i], scores, 0.0), axis=1, keepdims=True)
        s_hi = jnp.sum(jnp.where(lane == ehi_ref[i], scores, 0.0), axis=1, keepdims=True)
        tot = s_lo + s_hi

        def ffn(wg_ref, wu_ref, w_row):
            gate = jnp.dot(xs, wg_ref[0], preferred_element_type=F32)
            up = jnp.dot(xs, wu_ref[0], preferred_element_type=F32)
            return ((gate * jax.nn.sigmoid(gate)) * up * w_row).astype(BF16)

        y = jnp.dot(ffn(wgl_ref, wul_ref, s_lo / tot), wdl_ref[0], preferred_element_type=F32)
        y = y + jnp.dot(ffn(wgh_ref, wuh_ref, s_hi / tot), wdh_ref[0], preferred_element_type=F32)
        y_ref[...] = y.reshape(y_ref.shape)

    @pl.when(i >= nused_ref[0])
    def _():
        y_ref[...] = jnp.zeros_like(y_ref)


def _moe(tile_lo, tile_hi, n_used, x_sorted, rw, wg, wu, wd, tmo):
    n_tiles = x_sorted.shape[0] // tmo
    d, de = wg.shape[1], wg.shape[2]
    lo3 = lambda i, elo, ehi, nu: (elo[i], 0, 0)
    hi3 = lambda i, elo, ehi, nu: (ehi[i], 0, 0)
    grid_spec = pltpu.PrefetchScalarGridSpec(
        num_scalar_prefetch=3,
        grid=(n_tiles,),
        in_specs=[
            pl.BlockSpec((tmo,) + ROW_TILE, lambda i, elo, ehi, nu: (jnp.minimum(i, nu[0] - 1), 0, 0)),
            pl.BlockSpec((d, V7X_LANES), lambda i, elo, ehi, nu: (0, 0)),
            pl.BlockSpec((1, d, de), lo3), pl.BlockSpec((1, d, de), lo3), pl.BlockSpec((1, de, d), lo3),
            pl.BlockSpec((1, d, de), hi3), pl.BlockSpec((1, d, de), hi3), pl.BlockSpec((1, de, d), hi3),
        ],
        out_specs=pl.BlockSpec((tmo,) + ROW_TILE, lambda i, elo, ehi, nu: (i, 0, 0)),
    )
    return pl.pallas_call(
        _moe_kernel,
        out_shape=jax.ShapeDtypeStruct((n_tiles * tmo,) + ROW_TILE, F32),
        grid_spec=grid_spec,
        compiler_params=_cparams(("arbitrary",)),
        name="moe",
    )(tile_lo, tile_hi, n_used, x_sorted, rw, wg, wu, wd, wg, wu, wd)


def _start_row_gather(src_hbm, idx_ref, base, dst, sem, rows):
    def body(r, c):
        pltpu.make_async_copy(src_hbm.at[pl.ds(idx_ref[base + r], 1)], dst.at[pl.ds(r, 1)], sem).start()
        return c

    lax.fori_loop(0, rows, body, 0, unroll=GATHER_UNROLL)


def _wait_row_gather(src_hbm, dst, sem, rows):
    pltpu.make_async_copy(src_hbm.at[pl.ds(0, rows)], dst, sem).wait()


def _combine_kernel(pos_ref, x_ref, g_ref, y_hbm, o_ref, ybuf, sem, *, tmc):
    i = pl.program_id(0)
    last = pl.num_programs(0) - 1

    @pl.when(i == 0)
    def _():
        _start_row_gather(y_hbm, pos_ref, 0, ybuf.at[0], sem.at[0], tmc)

    @pl.when(i < last)
    def _():
        nxt = (i + 1) % 2
        _start_row_gather(y_hbm, pos_ref, (i + 1) * tmc, ybuf.at[nxt], sem.at[nxt], tmc)

    slot = i % 2
    _wait_row_gather(y_hbm, ybuf.at[slot], sem.at[slot], tmc)
    o_ref[...] = x_ref[...] + g_ref[0] * ybuf[slot].reshape(o_ref.shape)


def _combine(pos, x2d, gate, y_sorted, n, tmc):
    t, d = x2d.shape
    per_batch = n // tmc
    grid_spec = pltpu.PrefetchScalarGridSpec(
        num_scalar_prefetch=1,
        grid=(t // tmc,),
        in_specs=[
            pl.BlockSpec((tmc, d), lambda i, pos: (i, 0)),
            pl.BlockSpec((1, 1, d), lambda i, pos: (i // per_batch, 0, 0)),
            pl.BlockSpec(memory_space=pl.ANY),
        ],
        out_specs=pl.BlockSpec((tmc, d), lambda i, pos: (i, 0)),
        scratch_shapes=[pltpu.VMEM((2, tmc) + ROW_TILE, F32), pltpu.SemaphoreType.DMA((2,))],
    )
    return pl.pallas_call(
        functools.partial(_combine_kernel, tmc=tmc),
        out_shape=jax.ShapeDtypeStruct((t, d), F32),
        grid_spec=grid_spec,
        compiler_params=_cparams(("arbitrary",)),
        name="combine",
    )(pos, x2d, gate, y_sorted)


def _tiles(n):
    big = 512 if n % 512 == 0 else 256
    return dict(inproj=big, poolsgu=big, attn=big, merge=256)


def _project(x, lw, mod, tl):
    return _inproj(x, lw["norm1_g"], mod[0], mod[1], lw["w_puv"], lw["w_qkt"], tl["inproj"])


def _mixer(x, puv, yat, lw, mod, rw, tl):
    b, n, d = x.shape
    sh1, sc1, gt1, sh2, sc2, _ = mod
    yps = _poolsgu(puv, lw["pool_bd"], lw["pool_scale"], lw["sgu_norm_g"], lw["sgu_ws"], lw["sgu_bias"],
                   tl["poolsgu"])
    xmid, h2r, lg = _merge(x, lw["norm1_g"], sh1, sc1, gt1, lw["w_gate"], lw["b_gate"], yps, yat,
                           lw["w_br_pool"], lw["w_br_sgu"], lw["w_br_mla"], lw["w_out"],
                           lw["norm2_g"], sh2, sc2, rw, tl["merge"])
    return xmid.reshape(b * n, d), h2r.reshape((b * n,) + ROW_TILE), lg.reshape(b * n, V7X_LANES)


def _moe_block(streams, lw, rw, rb):
    counts = jnp.zeros((CLS_ROWS, V7X_LANES), F32)
    cls_l, rank_l = [], []
    for st in streams:
        cls, rank, counts = _route(st[2], rb, counts)
        cls_l.append(cls[0])
        rank_l.append(rank[0])
    t_all = sum(c.shape[0] for c in cls_l)
    pos, tile_lo, tile_hi, n_used, pad_start, pad_len = _plan(
        jnp.concatenate(cls_l), jnp.concatenate(rank_l), counts[:N_CLASSES, 0].astype(jnp.int32), MOE_TILE)
    n_rows = (t_all // MOE_TILE + N_CLASSES) * MOE_TILE
    x_sorted = _dispatch(pos, pad_start, pad_len, n_used, [st[1] for st in streams], n_rows)
    y_sorted = _moe(tile_lo, tile_hi, n_used, x_sorted, rw, lw["w_e_gate"], lw["w_e_up"], lw["w_e_down"], MOE_TILE)
    outs, first = [], 0
    for xmid, _, _, gate, n in streams:
        t = xmid.shape[0]
        outs.append(_combine(pos[first:first + t], xmid, gate, y_sorted, n, MOE_TILE))
        first += t
    return outs


def _rope_tables_t(n):
    rows = n // GRID_W
    row = jnp.repeat(jnp.arange(rows, dtype=F32), GRID_W)
    col = jnp.tile(jnp.arange(GRID_W, dtype=F32), rows)
    inv = ROPE_BASE ** (-(jnp.arange(ROPE_PAIRS, dtype=F32) * 2.0 / ROPE_AXIS))
    ang_r = inv[:, None] * row[None, :]
    ang_c = inv[:, None] * col[None, :]
    ang = jnp.concatenate([ang_r, ang_r, ang_c, ang_c], axis=0)
    return jnp.cos(ang), jnp.sin(ang)


def kernel(x, c, ctx, c_ctx, w_mod, b_mod, norm1_g, norm2_g, w_in, pool_w, pool_scale, sgu_norm_g, sgu_ws, sgu_b, qa_norm_g, w_uq, kva_norm_g, w_ukv, q_norm_g, k_norm_g, w_br_pool, w_br_sgu, w_br_mla, b_gate, w_out, router_w, router_b, w_e_gate, w_e_up, w_e_down):
    bsz, n_lat, d = x.shape
    n_ctx = ctx.shape[1]
    depth = w_mod.shape[0]
    assert bsz + 1 <= V7X_SUBLANES

    cond8 = jnp.zeros((V7X_SUBLANES, d), F32).at[:bsz].set(c).at[bsz].set(c_ctx)
    mods = _adaln(cond8, w_mod, b_mod)
    cos_l, sin_l = _rope_tables_t(n_lat)
    cos_t = jnp.concatenate([cos_l, jnp.ones((QK_ROPE, n_ctx), F32)], axis=1)
    sin_t = jnp.concatenate([sin_l, jnp.zeros((QK_ROPE, n_ctx), F32)], axis=1)
    nl, nc = n_lat // KEY_TILE, n_ctx // KEY_TILE
    bro = lambda g: jnp.broadcast_to(g[:, None], (g.shape[0], KEY_TILE))
    rw_hi = router_w.astype(BF16)
    rw_lo = (router_w - rw_hi.astype(F32)).astype(BF16)
    rw = jnp.concatenate([rw_hi, rw_lo, jnp.zeros((d, V7X_LANES - 2 * N_EXPERTS), BF16)], axis=1)
    tl_lat, tl_ctx = _tiles(n_lat), _tiles(n_ctx)

    x_lat, x_ctx = x, ctx
    for l in range(depth):
        last = l == depth - 1
        wl = w_in[l]
        lw = dict(
            norm1_g=norm1_g[l][None], norm2_g=norm2_g[l][None],
            w_puv=wl[:, :PUV_WIDTH].astype(BF16),
            w_qkt=wl[:, OFF_QA:OFF_GATE].T.astype(BF16),
            w_gate=wl[:, OFF_GATE:].astype(BF16), b_gate=b_gate[l][None],
            pool_bd=jax.scipy.linalg.block_diag(*[pool_w[l, g] for g in range(len(POOL_WINDOWS))]).astype(BF16),
            pool_scale=pool_scale[l][None], sgu_norm_g=sgu_norm_g[l][None],
            sgu_ws=sgu_ws[l].astype(BF16), sgu_bias=jnp.repeat(sgu_b[l].T, SGU_GROUP, axis=1),
            w_uq_t=w_uq[l].T.astype(BF16), w_ukv_t=w_ukv[l].T.astype(BF16),
            w_br_pool=w_br_pool[l].astype(BF16), w_br_sgu=w_br_sgu[l].astype(BF16),
            w_br_mla=w_br_mla[l].astype(BF16), w_out=w_out[l].astype(BF16),
            w_e_gate=w_e_gate[l].astype(BF16), w_e_up=w_e_up[l].astype(BF16), w_e_down=w_e_down[l].astype(BF16),
        )
        m = mods[l]
        mod_lat = [m[:bsz, i * d:(i + 1) * d][:, None, :] for i in range(6)]
        mod_ctx = [jnp.broadcast_to(m[bsz:bsz + 1, i * d:(i + 1) * d][:, None, :], (bsz, 1, d)) for i in range(6)]

        puv_c, qkt_c = _project(x_ctx, lw, mod_ctx, tl_ctx)
        puv, qkt = _project(x_lat, lw, mod_lat, tl_lat)
        qt, k, vt = _qkv(qkt, qkt_c, bro(qa_norm_g[l]), bro(kva_norm_g[l]), bro(q_norm_g[l]), bro(k_norm_g[l]),
                         cos_t, sin_t, lw["w_uq_t"], lw["w_ukv_t"])
        yat = _attention(qt, k, vt, 0, n_lat // tl_lat["attn"], tl_lat["attn"], 0, nl + nc)
        streams = [_mixer(x_lat, puv, yat, lw, mod_lat, rw, tl_lat) + (mod_lat[5], n_lat)]
        if not last:
            yat_c = _attention(qt, k, vt, nl, nc, KEY_TILE, nl, nc)
            streams.append(_mixer(x_ctx, puv_c, yat_c, lw, mod_ctx, rw, tl_ctx) + (mod_ctx[5], n_ctx))
        outs = _moe_block(streams, lw, rw, router_b)
        x_lat = outs[0].reshape(bsz, n_lat, d)
        if not last:
            x_ctx = outs[1].reshape(bsz, n_ctx, d)
    return x_lat
```

```python
import functools
import math

import jax
import jax.numpy as jnp
from jax import lax
from jax.experimental import pallas as pl
from jax.experimental.pallas import tpu as pltpu

F32 = jnp.float32
BF16 = jnp.bfloat16

GRID_W = 64
POOL_WINDOWS = (2, 4, 8, 16)
POOL_WIDTH = 256
POOL_GROUP = POOL_WIDTH // len(POOL_WINDOWS)
SGU_WIDTH = 256
SGU_GROUPS = 4
SGU_GROUP = SGU_WIDTH // SGU_GROUPS
CHUNK = 128
HEADS = 8
QK_NOPE = 64
QK_ROPE = 32
V_HEAD = 64
Q_LORA = 384
KV_LORA = 256
QK_HEAD = QK_NOPE + QK_ROPE
MLA_WIDTH = HEADS * V_HEAD
SM_SCALE = QK_HEAD ** -0.5
ROPE_BASE = 10000.0
ROPE_AXIS = QK_ROPE // 2
ROPE_PAIRS = ROPE_AXIS // 2
N_BRANCH = 3
OFF_U = POOL_WIDTH
OFF_V = OFF_U + SGU_WIDTH
OFF_QA = OFF_V + SGU_WIDTH
OFF_KVA = OFF_QA + Q_LORA
OFF_KR = OFF_KVA + KV_LORA
OFF_GATE = OFF_KR + QK_ROPE
N_EXPERTS = 16
N_GROUPS = 4
EXPERTS_PER_GROUP = N_EXPERTS // N_GROUPS
PAIRS_PER_GROUP = EXPERTS_PER_GROUP * (EXPERTS_PER_GROUP - 1) // 2
N_CLASSES = N_GROUPS * PAIRS_PER_GROUP
EPS = 1e-6

PUV_WIDTH = OFF_QA
QKT_ROWS = OFF_GATE - OFF_QA
HEAD_PAD = 128
QK_EXP2_SCALE = SM_SCALE * math.log2(math.e)
EXP2_SAFE_RANGE = 100.0

V7X_LANES = 128
V7X_SUBLANES = 8
V7X_VMEM_LIMIT = 52 * 1024 * 1024
KEY_TILE = 256
MOE_TILE = 256
ROW_TILE = (V7X_SUBLANES, V7X_LANES)


def _cparams(sem, vmem=V7X_VMEM_LIMIT):
    return pltpu.CompilerParams(dimension_semantics=sem, vmem_limit_bytes=vmem)


def _norm_mod(x, g, shift, scale):
    ms = jnp.mean(x * x, axis=-1, keepdims=True)
    y = x * lax.rsqrt(ms + EPS) * g
    return y * (1.0 + scale) + shift


def _rms_rows(x, g):
    ms = jnp.mean(x * x, axis=0, keepdims=True)
    return x * lax.rsqrt(ms + EPS) * g


def _adaln_kernel(c_ref, w_ref, b_ref, o_ref):
    c = c_ref[...]
    s = (c * jax.nn.sigmoid(c)).astype(BF16)
    o_ref[0] = jnp.dot(s, w_ref[0].astype(BF16), preferred_element_type=F32) + b_ref[0]


def _adaln(cond8, w_mod, b_mod):
    depth, d, n6 = w_mod.shape
    tn = 1536
    return pl.pallas_call(
        _adaln_kernel,
        out_shape=jax.ShapeDtypeStruct((depth, V7X_SUBLANES, n6), F32),
        grid=(depth, n6 // tn),
        in_specs=[
            pl.BlockSpec((V7X_SUBLANES, d), lambda l, j: (0, 0)),
            pl.BlockSpec((1, d, tn), lambda l, j: (l, 0, j)),
            pl.BlockSpec((1, 1, tn), lambda l, j: (l, 0, j)),
        ],
        out_specs=pl.BlockSpec((1, V7X_SUBLANES, tn), lambda l, j: (l, 0, j)),
        compiler_params=_cparams(("parallel", "parallel")),
        name="adaln",
    )(cond8, w_mod, b_mod.reshape(depth, 1, n6))


def _inproj_kernel(x_ref, g_ref, sh_ref, sc_ref, wn_ref, wt_ref, puv_ref, qkt_ref):
    h = _norm_mod(x_ref[0], g_ref[...], sh_ref[0], sc_ref[0]).astype(BF16)
    puv_ref[0] = jnp.dot(h, wn_ref[...], preferred_element_type=F32)
    qkt_ref[0] = lax.dot_general(wt_ref[...], h, (((1,), (1,)), ((), ())),
                                 preferred_element_type=F32)


def _inproj(x, g, shift, scale, w_nat, w_t, tm):
    b, n, d = x.shape
    return pl.pallas_call(
        _inproj_kernel,
        out_shape=(jax.ShapeDtypeStruct((b, n, PUV_WIDTH), F32),
                   jax.ShapeDtypeStruct((b, QKT_ROWS, n), F32)),
        grid=(b, n // tm),
        in_specs=[
            pl.BlockSpec((1, tm, d), lambda bi, i: (bi, i, 0)),
            pl.BlockSpec((1, d), lambda bi, i: (0, 0)),
            pl.BlockSpec((1, 1, d), lambda bi, i: (bi, 0, 0)),
            pl.BlockSpec((1, 1, d), lambda bi, i: (bi, 0, 0)),
            pl.BlockSpec((d, PUV_WIDTH), lambda bi, i: (0, 0)),
            pl.BlockSpec((QKT_ROWS, d), lambda bi, i: (0, 0)),
        ],
        out_specs=(pl.BlockSpec((1, tm, PUV_WIDTH), lambda bi, i: (bi, i, 0)),
                   pl.BlockSpec((1, QKT_ROWS, tm), lambda bi, i: (bi, 0, i))),
        compiler_params=_cparams(("parallel", "parallel")),
        name="inproj",
    )(x, g, shift, scale, w_nat, w_t)


POOL_HALO = 8


def _poolsgu_kernel(puv_ref, prev_ref, next_ref, poolw_ref, pscale_ref, sgug_ref, ws_ref,
                    bias_ref, o_ref, *, n, tc):
    i = pl.program_id(1)
    last = pl.num_programs(1) - 1
    puv = puv_ref[0]
    p = puv[:, :POOL_WIDTH]
    prev = jnp.where(i > 0, prev_ref[0], 0.0)
    nxt = jnp.where(i < last, next_ref[0], 0.0)
    e = jnp.concatenate([prev, p, nxt], axis=0)
    rows = tc + 2 * POOL_HALO
    a2 = e + pltpu.roll(e, 1, 0)
    a4 = a2 + pltpu.roll(a2, 2, 0)
    a8 = a4 + pltpu.roll(a4, 4, 0)
    a16 = a8 + pltpu.roll(a8, 8, 0)
    w2 = a2[POOL_HALO:POOL_HALO + tc]
    w4 = pltpu.roll(a4, rows - 1, 0)[POOL_HALO:POOL_HALO + tc]
    w8 = pltpu.roll(a8, rows - 3, 0)[POOL_HALO:POOL_HALO + tc]
    w16 = pltpu.roll(a16, rows - 7, 0)[POOL_HALO:POOL_HALO + tc]
    lane = lax.broadcasted_iota(jnp.int32, (tc, POOL_WIDTH), 1)
    t = i * tc + lax.broadcasted_iota(jnp.int32, (tc, POOL_WIDTH), 0)
    grp = lane // POOL_GROUP
    half = jnp.where(grp == 0, 1, jnp.where(grp == 1, 2, jnp.where(grp == 2, 4, 8)))
    cnt = (jnp.minimum(t + half, n) - jnp.maximum(t - half, 0)).astype(F32)
    wsum = jnp.where(grp == 0, w2, jnp.where(grp == 1, w4, jnp.where(grp == 2, w8, w16)))
    dlt = (wsum / cnt - p).astype(BF16)
    y_pool = jnp.dot(dlt, poolw_ref[...], preferred_element_type=F32) * pscale_ref[...]
    o_ref[0, :, 0:POOL_WIDTH] = y_pool.astype(o_ref.dtype)

    u = jax.nn.gelu(puv[:, OFF_U:OFF_V])
    gv = jax.nn.gelu(puv[:, OFF_V:OFF_QA])
    vn = gv * lax.rsqrt(jnp.mean(gv * gv, axis=-1, keepdims=True) + EPS) * sgug_ref[...]
    clane = lax.broadcasted_iota(jnp.int32, (CHUNK, SGU_WIDTH), 1) // SGU_GROUP
    for c in range(tc // CHUNK):
        vc = vn[c * CHUNK:(c + 1) * CHUNK]
        mixed = bias_ref[...]
        for gi in range(SGU_GROUPS):
            vm = jnp.where(clane == gi, vc, 0.0).astype(BF16)
            mixed = mixed + jnp.dot(ws_ref[gi], vm, preferred_element_type=F32)
        y = u[c * CHUNK:(c + 1) * CHUNK] * mixed
        o_ref[0, c * CHUNK:(c + 1) * CHUNK, POOL_WIDTH:POOL_WIDTH + SGU_WIDTH] = y.astype(o_ref.dtype)


def _poolsgu(puv, poolw, pscale, sgug, ws, bias, tc):
    b, n, _ = puv.shape
    nh = n // POOL_HALO
    per = tc // POOL_HALO
    kern = functools.partial(_poolsgu_kernel, n=n, tc=tc)
    return pl.pallas_call(
        kern,
        out_shape=jax.ShapeDtypeStruct((b, n, POOL_WIDTH + SGU_WIDTH), BF16),
        grid=(b, n // tc),
        in_specs=[
            pl.BlockSpec((1, tc, PUV_WIDTH), lambda bi, i: (bi, i, 0)),
            pl.BlockSpec((1, POOL_HALO, POOL_WIDTH),
                         lambda bi, i: (bi, jnp.maximum(i * per - 1, 0), 0)),
            pl.BlockSpec((1, POOL_HALO, POOL_WIDTH),
                         lambda bi, i: (bi, jnp.minimum((i + 1) * per, nh - 1), 0)),
            pl.BlockSpec((POOL_WIDTH, POOL_WIDTH), lambda bi, i: (0, 0)),
            pl.BlockSpec((1, POOL_WIDTH), lambda bi, i: (0, 0)),
            pl.BlockSpec((1, SGU_WIDTH), lambda bi, i: (0, 0)),
            pl.BlockSpec((SGU_GROUPS, CHUNK, CHUNK), lambda bi, i: (0, 0, 0)),
            pl.BlockSpec((CHUNK, SGU_WIDTH), lambda bi, i: (0, 0)),
        ],
        out_specs=pl.BlockSpec((1, tc, POOL_WIDTH + SGU_WIDTH), lambda bi, i: (bi, i, 0)),
        compiler_params=_cparams(("parallel", "parallel")),
        name="poolsgu",
    )(puv, puv, puv, poolw, pscale, sgug, ws, bias)


V_ROWS = 80


def _qkv_kernel(lat_ref, ctx_ref, gqa_ref, gkva_ref, gq_ref, gk_ref, cos_ref, sin_ref, qpad_ref, wq_ref, wkv_ref,
                qt_ref, k_ref, vt_ref, *, n_lat_tiles):
    tm = KEY_TILE
    blk = jnp.where(pl.program_id(1) < n_lat_tiles, lat_ref[0], ctx_ref[0])
    qa_n = _rms_rows(blk[0:Q_LORA], gqa_ref[...]).astype(BF16)
    kva_n = _rms_rows(blk[Q_LORA:Q_LORA + KV_LORA], gkva_ref[...]).astype(BF16)
    q = jnp.dot(wq_ref[...], qa_n, preferred_element_type=F32)
    kv = jnp.dot(wkv_ref[...], kva_n, preferred_element_type=F32)
    gq = gq_ref[...]
    gk = gk_ref[...]

    def rope(x):
        r = ROPE_PAIRS
        rot = jnp.concatenate([-x[r:2 * r], x[0:r], -x[3 * r:4 * r], x[2 * r:3 * r]], axis=0)
        return x * cos_ref[...] + rot * sin_ref[...]

    k_rope = rope(_rms_rows(blk[Q_LORA + KV_LORA:QKT_ROWS], gk[QK_NOPE:QK_HEAD]))
    first_pad = lax.broadcasted_iota(jnp.int32, (HEAD_PAD - QK_HEAD, tm), 0) == 0
    kpad = jnp.where(first_pad, 1.0, 0.0)
    qpad = qpad_ref[...]
    ones_rows = jnp.where(lax.broadcasted_iota(jnp.int32, (V_ROWS - V_HEAD, tm), 0) == 0, 1.0, 0.0)
    kv_rows = QK_NOPE + V_HEAD
    for h in range(HEADS):
        qh = q[h * QK_HEAD:(h + 1) * QK_HEAD]
        q_nope = _rms_rows(qh[0:QK_NOPE], gq[0:QK_NOPE])
        q_rope = rope(_rms_rows(qh[QK_NOPE:QK_HEAD], gq[QK_NOPE:QK_HEAD]))
        qcat = jnp.concatenate([q_nope * QK_EXP2_SCALE, q_rope * QK_EXP2_SCALE, qpad], axis=0)
        qt_ref[0, h] = qcat.astype(qt_ref.dtype)
        k_nope = _rms_rows(kv[h * kv_rows:h * kv_rows + QK_NOPE], gk[0:QK_NOPE])
        kcat = jnp.concatenate([k_nope, k_rope, kpad], axis=0)
        k_ref[0, h, 0] = kcat.T.astype(k_ref.dtype)
        v = jnp.concatenate([kv[h * kv_rows + QK_NOPE:(h + 1) * kv_rows], ones_rows], axis=0)
        vt_ref[0, h, 0] = v.astype(vt_ref.dtype)


def _qkv(qkt_lat, qkt_ctx, gqa, gkva, gq, gk, cos_t, sin_t, qpad, wq_t, wkv_t):
    b, _, n_lat = qkt_lat.shape
    n_ctx = qkt_ctx.shape[2]
    tm = KEY_TILE
    nl, nc = n_lat // tm, n_ctx // tm
    nt = nl + nc
    kern = functools.partial(_qkv_kernel, n_lat_tiles=nl)
    full = lambda shape: pl.BlockSpec(shape, lambda bi, i: (0,) * len(shape))
    return pl.pallas_call(
        kern,
        out_shape=(jax.ShapeDtypeStruct((b, HEADS, HEAD_PAD, nt * tm), BF16),
                   jax.ShapeDtypeStruct((b, HEADS, nt, tm, HEAD_PAD), BF16),
                   jax.ShapeDtypeStruct((b, HEADS, nt, V_ROWS, tm), BF16)),
        grid=(b, nt),
        in_specs=[
            pl.BlockSpec((1, QKT_ROWS, tm), lambda bi, i: (bi, 0, jnp.minimum(i, nl - 1))),
            pl.BlockSpec((1, QKT_ROWS, tm), lambda bi, i: (bi, 0, jnp.maximum(i - nl, 0))),
            full((Q_LORA, tm)), full((KV_LORA, tm)), full((QK_HEAD, tm)), full((QK_HEAD, tm)),
            pl.BlockSpec((QK_ROPE, tm), lambda bi, i: (0, i)),
            pl.BlockSpec((QK_ROPE, tm), lambda bi, i: (0, i)),
            full((HEAD_PAD - QK_HEAD, tm)),
            full((HEADS * QK_HEAD, Q_LORA)), full((HEADS * (QK_NOPE + V_HEAD), KV_LORA)),
        ],
        out_specs=(pl.BlockSpec((1, HEADS, HEAD_PAD, tm), lambda bi, i: (bi, 0, 0, i)),
                   pl.BlockSpec((1, HEADS, 1, tm, HEAD_PAD), lambda bi, i: (bi, 0, i, 0, 0)),
                   pl.BlockSpec((1, HEADS, 1, V_ROWS, tm), lambda bi, i: (bi, 0, i, 0, 0))),
        compiler_params=_cparams(("parallel", "parallel")),
        name="qkv",
    )(qkt_lat, qkt_ctx, gqa, gkva, gq, gk, cos_t, sin_t, qpad, wq_t, wkv_t)


def _attn_kernel(qt_ref, k_ref, vt_ref, o_ref, m_ref, acc_ref, s_ref, *, g_n, s_n, per, online):
    if online:
        m_ref[...] = jnp.full(m_ref.shape, -jnp.inf, F32)
    acc_ref[...] = jnp.zeros(acc_ref.shape, F32)

    def qk(g, s, j, slot):
        s_ref[slot, g * s_n + s] = jnp.dot(k_ref[0, g, s * per + j], qt_ref[0, g],
                                            preferred_element_type=F32)

    def soft_pv(g, s, j, slot):
        ci = g * s_n + s
        sc = s_ref[slot, ci]
        vt = vt_ref[0, g, s * per + j]
        if online:
            m = m_ref[ci]
            m_new = jnp.maximum(m, jnp.max(sc, axis=0, keepdims=True))
            p = jnp.exp2(sc - m_new).astype(BF16)
            acc_ref[ci] = jnp.exp2(m - m_new) * acc_ref[ci] + jnp.dot(vt, p, preferred_element_type=F32)
            m_ref[ci] = m_new
        else:
            acc_ref[ci] += jnp.dot(vt, jnp.exp2(sc).astype(BF16), preferred_element_type=F32)

    for g in range(g_n):
        for s in range(s_n):
            qk(g, s, 0, 0)

    def body(jj, c):
        for half in range(2):
            j = 2 * jj + half
            for g in range(g_n):
                for s in range(s_n):
                    qk(g, s, j + 1, 1 - half)
                    soft_pv(g, s, j, half)
        return c

    if per > 1:
        lax.fori_loop(0, (per - 1) // 2, body, 0)
    for g in range(g_n):
        for s in range(s_n):
            soft_pv(g, s, per - 1, (per - 1) % 2)
    for g in range(g_n):
        tot = None
        if online:
            m = m_ref[g * s_n]
            for s in range(1, s_n):
                m = jnp.maximum(m, m_ref[g * s_n + s])
        for s in range(s_n):
            part = acc_ref[g * s_n + s]
            if online:
                part = part * jnp.exp2(m_ref[g * s_n + s] - m)
            tot = part if tot is None else tot + part
        o_ref[0, g * V_HEAD:(g + 1) * V_HEAD] = (tot[0:V_HEAD] / tot[V_HEAD:V_HEAD + 1]).astype(o_ref.dtype)


ATTN_HEADS_PER_STEP = 2


def _attention(qt, k, vt, q0, nq, tq, key0, nkeys, online):
    b = qt.shape[0]
    g_n = ATTN_HEADS_PER_STEP
    s_n = next(s for s in (3, 1) if nkeys % s == 0 and (nkeys // s - 1) % 2 == 0)
    assert key0 % nkeys == 0 and HEADS % g_n == 0
    kb = key0 // nkeys
    kern = functools.partial(_attn_kernel, g_n=g_n, s_n=s_n, per=nkeys // s_n, online=online)
    return pl.pallas_call(
        kern,
        out_shape=jax.ShapeDtypeStruct((b, MLA_WIDTH, nq * tq), BF16),
        grid=(b, HEADS // g_n, nq),
        in_specs=[
            pl.BlockSpec((1, g_n, HEAD_PAD, tq), lambda bi, hg, i: (bi, hg, 0, q0 + i)),
            pl.BlockSpec((1, g_n, nkeys, KEY_TILE, HEAD_PAD), lambda bi, hg, i: (bi, hg, kb, 0, 0)),
            pl.BlockSpec((1, g_n, nkeys, V_ROWS, KEY_TILE), lambda bi, hg, i: (bi, hg, kb, 0, 0)),
        ],
        out_specs=pl.BlockSpec((1, g_n * V_HEAD, tq), lambda bi, hg, i: (bi, hg, i)),
        scratch_shapes=[pltpu.VMEM((g_n * s_n, 1, tq), F32),
                        pltpu.VMEM((g_n * s_n, V_ROWS, tq), F32),
                        pltpu.VMEM((2, g_n * s_n, KEY_TILE, tq), F32)],
        compiler_params=_cparams(("parallel", "parallel", "parallel")),
        name="attention",
    )(qt, k, vt)


def _router_logits(h2, rw):
    hi = h2.astype(BF16)
    lo = (h2 - hi.astype(F32)).astype(BF16)
    r_hi = jnp.dot(hi, rw, preferred_element_type=F32)
    r_lo = jnp.dot(lo, rw, preferred_element_type=F32)
    lg = r_hi + pltpu.roll(r_hi, V7X_LANES - N_EXPERTS, 1) + r_lo
    lane = lax.broadcasted_iota(jnp.int32, lg.shape, 1)
    return jnp.where(lane < N_EXPERTS, lg, 0.0)


def _merge_kernel(x_ref, g1n_ref, sh1_ref, sc1_ref, gt1_ref, wg_ref, bg_ref, yps_ref, yat_ref,
                  wbp_ref, wbs_ref, wbm_ref, wout_ref, g2n_ref, sh2_ref, sc2_ref, rw_ref,
                  xmid_ref, h2_ref, lg_ref):
    x = x_ref[0]
    d = x.shape[1]
    h = _norm_mod(x, g1n_ref[...], sh1_ref[0], sc1_ref[0]).astype(BF16)
    yps = yps_ref[0]
    branches = (
        jnp.dot(yps[:, 0:POOL_WIDTH], wbp_ref[...], preferred_element_type=F32),
        jnp.dot(yps[:, POOL_WIDTH:POOL_WIDTH + SGU_WIDTH], wbs_ref[...], preferred_element_type=F32),
        lax.dot_general(yat_ref[0], wbm_ref[...], (((0,), (0,)), ((), ())),
                        preferred_element_type=F32),
    )
    merged = None
    for i in range(N_BRANCH):
        gl = jnp.dot(h, wg_ref[:, i * d:(i + 1) * d], preferred_element_type=F32) + bg_ref[:, i * d:(i + 1) * d]
        term = jax.nn.sigmoid(gl) * branches[i]
        merged = term if merged is None else merged + term
    out = jnp.dot(merged.astype(BF16), wout_ref[...], preferred_element_type=F32)
    xm = x + gt1_ref[0] * out
    xmid_ref[0] = xm
    h2 = _norm_mod(xm, g2n_ref[...], sh2_ref[0], sc2_ref[0])
    h2_ref[0] = h2.reshape((h2.shape[0],) + ROW_TILE)
    lg_ref[0] = _router_logits(h2, rw_ref[...])


def _merge(x, g1n, sh1, sc1, gt1, wg, bg, yps, yat, wbp, wbs, wbm, wout, g2n, sh2, sc2, rw, tm):
    b, n, d = x.shape
    row = lambda: pl.BlockSpec((1, 1, d), lambda bi, i: (bi, 0, 0))
    full = lambda shape: pl.BlockSpec(shape, lambda bi, i: (0,) * len(shape))
    return pl.pallas_call(
        _merge_kernel,
        out_shape=(jax.ShapeDtypeStruct((b, n, d), F32),
                   jax.ShapeDtypeStruct((b, n) + ROW_TILE, F32),
                   jax.ShapeDtypeStruct((b, n, V7X_LANES), F32)),
        grid=(b, n // tm),
        in_specs=[
            pl.BlockSpec((1, tm, d), lambda bi, i: (bi, i, 0)),
            full((1, d)), row(), row(), row(),
            full((d, N_BRANCH * d)), full((1, N_BRANCH * d)),
            pl.BlockSpec((1, tm, POOL_WIDTH + SGU_WIDTH), lambda bi, i: (bi, i, 0)),
            pl.BlockSpec((1, MLA_WIDTH, tm), lambda bi, i: (bi, 0, i)),
            full((POOL_WIDTH, d)), full((SGU_WIDTH, d)), full((MLA_WIDTH, d)), full((d, d)),
            full((1, d)), row(), row(),
            full((d, V7X_LANES)),
        ],
        out_specs=(pl.BlockSpec((1, tm, d), lambda bi, i: (bi, i, 0)),
                   pl.BlockSpec((1, tm) + ROW_TILE, lambda bi, i: (bi, i, 0, 0)),
                   pl.BlockSpec((1, tm, V7X_LANES), lambda bi, i: (bi, i, 0))),
        compiler_params=_cparams(("parallel", "parallel")),
        name="merge",
    )(x, g1n, sh1, sc1, gt1, wg, bg, yps, yat, wbp, wbs, wbm, wout, g2n, sh2, sc2, rw)


CLS_ROWS = 32


def _route_kernel(ext_ref, rb_ref, tri_ref, init_ref, cls_ref, rank_ref, cnt_ref, carry_ref):
    tn = ext_ref.shape[0]

    @pl.when(pl.program_id(0) == 0)
    def _():
        carry_ref[...] = jnp.broadcast_to(init_ref[:, 0:1], carry_ref.shape)

    scores = jax.nn.sigmoid(ext_ref[...].T[0:N_EXPERTS])
    sel = scores + rb_ref[...]
    best = None
    for g in range(N_GROUPS):
        rows = [sel[g * EXPERTS_PER_GROUP + j:g * EXPERTS_PER_GROUP + j + 1] for j in range(EXPERTS_PER_GROUP)]
        v1, i1 = rows[0], jnp.zeros_like(rows[0], jnp.int32)
        for j in range(1, EXPERTS_PER_GROUP):
            gt = rows[j] > v1
            v1 = jnp.where(gt, rows[j], v1)
            i1 = jnp.where(gt, j, i1)
        v2 = jnp.full_like(v1, -jnp.inf)
        i2 = jnp.zeros_like(i1)
        for j in range(EXPERTS_PER_GROUP):
            gt = (i1 != j) & (rows[j] > v2)
            v2 = jnp.where(gt, rows[j], v2)
            i2 = jnp.where(gt, j, i2)
        gs = v1 + v2
        if best is None:
            best = (gs, jnp.full_like(i1, g), i1, i2)
        else:
            gt = gs > best[0]
            cand = (gs, jnp.full_like(i1, g), i1, i2)
            best = tuple(jnp.where(gt, c, b) for c, b in zip(cand, best))
    _, grp, i1, i2 = best
    lo = jnp.minimum(i1, i2)
    hi = jnp.maximum(i1, i2)
    base = jnp.where(lo == 0, 0, jnp.where(lo == 1, 3, 5))
    cls = grp * PAIRS_PER_GROUP + base + (hi - lo - 1)
    cls_ref[...] = cls
    onehot = jnp.where(lax.broadcasted_iota(jnp.int32, (CLS_ROWS, tn), 0) == cls, 1.0, 0.0)
    cum = jnp.dot(onehot.astype(BF16), tri_ref[...], preferred_element_type=F32)
    carry = carry_ref[...]
    rank_ref[...] = (jnp.sum(onehot * (carry + cum), axis=0, keepdims=True) - 1.0).astype(jnp.int32)
    carry = carry + jnp.broadcast_to(cum[:, tn - 1:tn], carry.shape)
    carry_ref[...] = carry
    cnt_ref[...] = carry[:, 0:V7X_LANES]


def _route(logits, rb, init_counts):
    t = logits.shape[0]
    tn = next(w for w in (512, 256) if t % w == 0)
    tri = (jnp.arange(tn)[:, None] <= jnp.arange(tn)[None, :]).astype(BF16)
    return pl.pallas_call(
        _route_kernel,
        out_shape=(jax.ShapeDtypeStruct((1, t), jnp.int32), jax.ShapeDtypeStruct((1, t), jnp.int32),
                   jax.ShapeDtypeStruct((CLS_ROWS, V7X_LANES), F32)),
        grid=(t // tn,),
        in_specs=[pl.BlockSpec((tn, V7X_LANES), lambda i: (i, 0)),
                  pl.BlockSpec((N_EXPERTS, tn), lambda i: (0, 0)),
                  pl.BlockSpec((tn, tn), lambda i: (0, 0)),
                  pl.BlockSpec((CLS_ROWS, V7X_LANES), lambda i: (0, 0))],
        out_specs=(pl.BlockSpec((1, tn), lambda i: (0, i)), pl.BlockSpec((1, tn), lambda i: (0, i)),
                   pl.BlockSpec((CLS_ROWS, V7X_LANES), lambda i: (0, 0))),
        scratch_shapes=[pltpu.VMEM((CLS_ROWS, tn), F32)],
        compiler_params=_cparams(("arbitrary",)),
        name="route",
    )(logits, jnp.broadcast_to(rb[:, None], (N_EXPERTS, tn)), tri, init_counts)


def _plan(cls, rank, counts, tmo):
    t = cls.shape[0]
    n_tiles = t // tmo + N_CLASSES
    class_ids = jnp.arange(N_CLASSES, dtype=jnp.int32)
    tiles = (counts + tmo - 1) // tmo
    tile_end = jnp.cumsum(tiles)
    tile_start = tile_end - tiles
    n_used = tile_end[-1]
    pos = rank + jnp.sum(jnp.where(cls[None, :] == class_ids[:, None], (tile_start * tmo)[:, None], 0), axis=0)
    tile_id = jnp.arange(n_tiles, dtype=jnp.int32)
    tile_cls = jnp.sum(tile_id[:, None] >= tile_end[None, :], axis=1).astype(jnp.int32)
    last_cls = jnp.max(jnp.where(counts > 0, class_ids, 0))
    tile_cls = jnp.where(tile_id < n_used, tile_cls, last_cls)
    grp = tile_cls // PAIRS_PER_GROUP
    pair = tile_cls % PAIRS_PER_GROUP
    lo = jnp.where(pair < 3, 0, jnp.where(pair < 5, 1, 2))
    hi = jnp.where(pair < 3, pair + 1, jnp.where(pair < 5, pair - 1, 3))
    tile_lo = (grp * EXPERTS_PER_GROUP + lo).astype(jnp.int32)
    tile_hi = (grp * EXPERTS_PER_GROUP + hi).astype(jnp.int32)
    pad_start = (tile_start * tmo + counts).astype(jnp.int32)
    pad_len = (tiles * tmo - counts).astype(jnp.int32)
    return (pos.astype(jnp.int32), tile_lo, tile_hi, n_used.reshape(1).astype(jnp.int32), pad_start, pad_len)


GATHER_UNROLL = 8
DMA_PRIORITIES = 2
PAD_BITS = 8


def _dispatch_kernel(pos_ref, pstart_ref, plen_ref, nused_ref, *refs, src_steps, rs):
    n_src = len(src_steps)
    srcs, out, zbuf, sem = refs[:n_src], refs[n_src], refs[n_src + 1], refs[n_src + 2]
    i = pl.program_id(0)
    last = pl.num_programs(0) - 1

    def pad_copies(act):
        for c in range(N_CLASSES):
            ln, st = plen_ref[c], pstart_ref[c]
            for k in range(PAD_BITS):
                size = 1 << k

                @pl.when((ln & size) != 0)
                def _(size=size, ln=ln, st=st):
                    off = st + (ln & (size - 1))
                    act(pltpu.make_async_copy(zbuf.at[pl.ds(0, size)], out.at[pl.ds(off, size)], sem.at[1]))
        half = zbuf.shape[0]
        n_tiles = out.shape[0] // (2 * half)
        for j in range(n_tiles - N_CLASSES, n_tiles):
            @pl.when(j >= nused_ref[0])
            def _(j=j):
                for part in range(2):
                    act(pltpu.make_async_copy(zbuf, out.at[pl.ds((2 * j + part) * half, half)], sem.at[1]))

    @pl.when(i == 0)
    def _():
        zbuf[...] = jnp.zeros_like(zbuf)
        pad_copies(lambda cp: cp.start())

    first = 0
    for src, steps in zip(srcs, src_steps):
        @pl.when((i >= first) & (i < first + steps))
        def _(src=src):
            def body(rr, c):
                for k in range(DMA_PRIORITIES):
                    r = DMA_PRIORITIES * rr + k
                    pltpu.make_async_copy(src.at[pl.ds(r, 1)], out.at[pl.ds(pos_ref[i * rs + r], 1)],
                                          sem.at[0]).start(priority=k)
                return c

            lax.fori_loop(0, rs // DMA_PRIORITIES, body, 0, unroll=GATHER_UNROLL // DMA_PRIORITIES)
            pltpu.make_async_copy(src, out.at[pl.ds(0, rs)], sem.at[0]).wait()
        first += steps

    @pl.when(i == last)
    def _():
        pad_copies(lambda cp: cp.wait())


def _dispatch(pos, pad_start, pad_len, n_used, sources, n_rows):
    rs = next(w for w in (512, 256) if all(s.shape[0] % w == 0 for s in sources))
    src_steps = tuple(s.shape[0] // rs for s in sources)
    firsts = [sum(src_steps[:k]) for k in range(len(sources))]

    def block(first, steps):
        return pl.BlockSpec((rs,) + ROW_TILE, lambda i, *_: (jnp.clip(i - first, 0, steps - 1), 0, 0))

    grid_spec = pltpu.PrefetchScalarGridSpec(
        num_scalar_prefetch=4,
        grid=(sum(src_steps),),
        in_specs=[block(f, st) for f, st in zip(firsts, src_steps)],
        out_specs=pl.BlockSpec(memory_space=pl.ANY),
        scratch_shapes=[pltpu.VMEM((1 << (PAD_BITS - 1),) + ROW_TILE, F32), pltpu.SemaphoreType.DMA((2,))],
    )
    return pl.pallas_call(
        functools.partial(_dispatch_kernel, src_steps=src_steps, rs=rs),
        out_shape=jax.ShapeDtypeStruct((n_rows,) + ROW_TILE, F32),
        grid_spec=grid_spec,
        compiler_params=_cparams(("arbitrary",)),
        name="dispatch",
    )(pos, pad_start, pad_len, n_used, *sources)


def _moe_kernel(elo_ref, ehi_ref, nused_ref, xs_ref, rw_ref, wgl_ref, wul_ref, wdl_ref,
                wgh_ref, wuh_ref, wdh_ref, y_ref):
    i = pl.program_id(0)
    tmo = xs_ref.shape[0]

    @pl.when(i < nused_ref[0])
    def _():
        x = xs_ref[...].reshape(tmo, -1)
        xs = x.astype(BF16)
        scores = jax.nn.sigmoid(jnp.dot(xs, rw_ref[...], preferred_element_type=F32))
        lane = lax.broadcasted_iota(jnp.int32, scores.shape, 1)
        s_lo = jnp.sum(jnp.where(lane == elo_ref[i], scores, 0.0), axis=1, keepdims=True)
        s_hi = jnp.sum(jnp.where(lane == ehi_ref[i], scores, 0.0), axis=1, keepdims=True)
        tot = s_lo + s_hi

        def ffn(wg_ref, wu_ref, w_row):
            gate = jnp.dot(xs, wg_ref[0], preferred_element_type=F32)
            up = jnp.dot(xs, wu_ref[0], preferred_element_type=F32)
            return ((gate * jax.nn.sigmoid(gate)) * up * w_row).astype(BF16)

        y = jnp.dot(ffn(wgl_ref, wul_ref, s_lo / tot), wdl_ref[0], preferred_element_type=F32)
        y = y + jnp.dot(ffn(wgh_ref, wuh_ref, s_hi / tot), wdh_ref[0], preferred_element_type=F32)
        y_ref[...] = y.reshape(y_ref.shape)

    @pl.when(i >= nused_ref[0])
    def _():
        y_ref[...] = jnp.zeros_like(y_ref)


def _moe(tile_lo, tile_hi, n_used, x_sorted, rw, wg, wu, wd, tmo):
    n_tiles = x_sorted.shape[0] // tmo
    d, de = wg.shape[1], wg.shape[2]
    lo3 = lambda i, elo, ehi, nu: (elo[i], 0, 0)
    hi3 = lambda i, elo, ehi, nu: (ehi[i], 0, 0)
    grid_spec = pltpu.PrefetchScalarGridSpec(
        num_scalar_prefetch=3,
        grid=(n_tiles,),
        in_specs=[
            pl.BlockSpec((tmo,) + ROW_TILE, lambda i, elo, ehi, nu: (jnp.minimum(i, nu[0] - 1), 0, 0)),
            pl.BlockSpec((d, V7X_LANES), lambda i, elo, ehi, nu: (0, 0)),
            pl.BlockSpec((1, d, de), lo3), pl.BlockSpec((1, d, de), lo3), pl.BlockSpec((1, de, d), lo3),
            pl.BlockSpec((1, d, de), hi3), pl.BlockSpec((1, d, de), hi3), pl.BlockSpec((1, de, d), hi3),
        ],
        out_specs=pl.BlockSpec((tmo,) + ROW_TILE, lambda i, elo, ehi, nu: (i, 0, 0)),
    )
    return pl.pallas_call(
        _moe_kernel,
        out_shape=jax.ShapeDtypeStruct((n_tiles * tmo,) + ROW_TILE, F32),
        grid_spec=grid_spec,
        compiler_params=_cparams(("arbitrary",)),
        name="moe",
    )(tile_lo, tile_hi, n_used, x_sorted, rw, wg, wu, wd, wg, wu, wd)


def _start_row_gather(src_hbm, idx_ref, base, dst, sem, rows):
    def body(rr, c):
        for k in range(DMA_PRIORITIES):
            r = DMA_PRIORITIES * rr + k
            pltpu.make_async_copy(src_hbm.at[pl.ds(idx_ref[base + r], 1)], dst.at[pl.ds(r, 1)],
                                  sem).start(priority=k)
        return c

    lax.fori_loop(0, rows // DMA_PRIORITIES, body, 0, unroll=GATHER_UNROLL // DMA_PRIORITIES)


def _wait_row_gather(src_hbm, dst, sem, rows):
    pltpu.make_async_copy(src_hbm.at[pl.ds(0, rows)], dst, sem).wait()


def _combine_kernel(pos_ref, x_ref, g_ref, y_hbm, o_ref, ybuf, sem, *, tmc):
    i = pl.program_id(0)
    last = pl.num_programs(0) - 1

    @pl.when(i == 0)
    def _():
        _start_row_gather(y_hbm, pos_ref, 0, ybuf.at[0], sem.at[0], tmc)

    @pl.when(i < last)
    def _():
        nxt = (i + 1) % 2
        _start_row_gather(y_hbm, pos_ref, (i + 1) * tmc, ybuf.at[nxt], sem.at[nxt], tmc)

    slot = i % 2
    _wait_row_gather(y_hbm, ybuf.at[slot], sem.at[slot], tmc)
    o_ref[...] = x_ref[...] + g_ref[0] * ybuf[slot].reshape(o_ref.shape)


def _combine(pos, x2d, gate, y_sorted, n, tmc):
    t, d = x2d.shape
    per_batch = n // tmc
    grid_spec = pltpu.PrefetchScalarGridSpec(
        num_scalar_prefetch=1,
        grid=(t // tmc,),
        in_specs=[
            pl.BlockSpec((tmc, d), lambda i, pos: (i, 0)),
            pl.BlockSpec((1, 1, d), lambda i, pos: (i // per_batch, 0, 0)),
            pl.BlockSpec(memory_space=pl.ANY),
        ],
        out_specs=pl.BlockSpec((tmc, d), lambda i, pos: (i, 0)),
        scratch_shapes=[pltpu.VMEM((2, tmc) + ROW_TILE, F32), pltpu.SemaphoreType.DMA((2,))],
    )
    return pl.pallas_call(
        functools.partial(_combine_kernel, tmc=tmc),
        out_shape=jax.ShapeDtypeStruct((t, d), F32),
        grid_spec=grid_spec,
        compiler_params=_cparams(("arbitrary",)),
        name="combine",
    )(pos, x2d, gate, y_sorted)


def _tiles(n):
    big = 512 if n % 512 == 0 else 256
    return dict(inproj=big, poolsgu=big, attn=big, merge=256)


def _project(x, lw, mod, tl):
    return _inproj(x, lw["norm1_g"], mod[0], mod[1], lw["w_puv"], lw["w_qkt"], tl["inproj"])


def _mixer(x, puv, yat, lw, mod, rw, tl):
    b, n, d = x.shape
    sh1, sc1, gt1, sh2, sc2, _ = mod
    yps = _poolsgu(puv, lw["pool_bd"], lw["pool_scale"], lw["sgu_norm_g"], lw["sgu_ws"], lw["sgu_bias"],
                   tl["poolsgu"])
    xmid, h2r, lg = _merge(x, lw["norm1_g"], sh1, sc1, gt1, lw["w_gate"], lw["b_gate"], yps, yat,
                           lw["w_br_pool"], lw["w_br_sgu"], lw["w_br_mla"], lw["w_out"],
                           lw["norm2_g"], sh2, sc2, rw, tl["merge"])
    return xmid.reshape(b * n, d), h2r.reshape((b * n,) + ROW_TILE), lg.reshape(b * n, V7X_LANES)


def _moe_block(streams, lw, rw, rb):
    counts = jnp.zeros((CLS_ROWS, V7X_LANES), F32)
    cls_l, rank_l = [], []
    for st in streams:
        cls, rank, counts = _route(st[2], rb, counts)
        cls_l.append(cls[0])
        rank_l.append(rank[0])
    t_all = sum(c.shape[0] for c in cls_l)
    pos, tile_lo, tile_hi, n_used, pad_start, pad_len = _plan(
        jnp.concatenate(cls_l), jnp.concatenate(rank_l), counts[:N_CLASSES, 0].astype(jnp.int32), MOE_TILE)
    n_rows = (t_all // MOE_TILE + N_CLASSES) * MOE_TILE
    x_sorted = _dispatch(pos, pad_start, pad_len, n_used, [st[1] for st in streams], n_rows)
    y_sorted = _moe(tile_lo, tile_hi, n_used, x_sorted, rw, lw["w_e_gate"], lw["w_e_up"], lw["w_e_down"], MOE_TILE)
    outs, first = [], 0
    for xmid, _, _, gate, n in streams:
        t = xmid.shape[0]
        outs.append(_combine(pos[first:first + t], xmid, gate, y_sorted, n, MOE_TILE))
        first += t
    return outs


def _rope_tables_t(n):
    rows = n // GRID_W
    row = jnp.repeat(jnp.arange(rows, dtype=F32), GRID_W)
    col = jnp.tile(jnp.arange(GRID_W, dtype=F32), rows)
    inv = ROPE_BASE ** (-(jnp.arange(ROPE_PAIRS, dtype=F32) * 2.0 / ROPE_AXIS))
    ang_r = inv[:, None] * row[None, :]
    ang_c = inv[:, None] * col[None, :]
    ang = jnp.concatenate([ang_r, ang_r, ang_c, ang_c], axis=0)
    return jnp.cos(ang), jnp.sin(ang)


def kernel(x, c, ctx, c_ctx, w_mod, b_mod, norm1_g, norm2_g, w_in, pool_w, pool_scale, sgu_norm_g, sgu_ws, sgu_b, qa_norm_g, w_uq, kva_norm_g, w_ukv, q_norm_g, k_norm_g, w_br_pool, w_br_sgu, w_br_mla, b_gate, w_out, router_w, router_b, w_e_gate, w_e_up, w_e_down):
    bsz, n_lat, d = x.shape
    n_ctx = ctx.shape[1]
    depth = w_mod.shape[0]
    assert bsz + 1 <= V7X_SUBLANES

    cond8 = jnp.zeros((V7X_SUBLANES, d), F32).at[:bsz].set(c).at[bsz].set(c_ctx)
    mods = _adaln(cond8, w_mod, b_mod)
    cos_l, sin_l = _rope_tables_t(n_lat)
    cos_t = jnp.concatenate([cos_l, jnp.ones((QK_ROPE, n_ctx), F32)], axis=1)
    sin_t = jnp.concatenate([sin_l, jnp.zeros((QK_ROPE, n_ctx), F32)], axis=1)
    nl, nc = n_lat // KEY_TILE, n_ctx // KEY_TILE
    bro = lambda g: jnp.broadcast_to(g[:, None], (g.shape[0], KEY_TILE))
    rw_hi = router_w.astype(BF16)
    rw_lo = (router_w - rw_hi.astype(F32)).astype(BF16)
    rw = jnp.concatenate([rw_hi, rw_lo, jnp.zeros((d, V7X_LANES - 2 * N_EXPERTS), BF16)], axis=1)
    tl_lat, tl_ctx = _tiles(n_lat), _tiles(n_ctx)

    x_lat, x_ctx = x, ctx
    for l in range(depth):
        last = l == depth - 1
        wl = w_in[l]
        lw = dict(
            norm1_g=norm1_g[l][None], norm2_g=norm2_g[l][None],
            w_puv=wl[:, :PUV_WIDTH].astype(BF16),
            w_qkt=wl[:, OFF_QA:OFF_GATE].T.astype(BF16),
            w_gate=wl[:, OFF_GATE:].astype(BF16), b_gate=b_gate[l][None],
            pool_bd=jax.scipy.linalg.block_diag(*[pool_w[l, g] for g in range(len(POOL_WINDOWS))]).astype(BF16),
            pool_scale=pool_scale[l][None], sgu_norm_g=sgu_norm_g[l][None],
            sgu_ws=sgu_ws[l].astype(BF16), sgu_bias=jnp.repeat(sgu_b[l].T, SGU_GROUP, axis=1),
            w_uq_t=w_uq[l].T.astype(BF16), w_ukv_t=w_ukv[l].T.astype(BF16),
            w_br_pool=w_br_pool[l].astype(BF16), w_br_sgu=w_br_sgu[l].astype(BF16),
            w_br_mla=w_br_mla[l].astype(BF16), w_out=w_out[l].astype(BF16),
            w_e_gate=w_e_gate[l].astype(BF16), w_e_up=w_e_up[l].astype(BF16), w_e_down=w_e_down[l].astype(BF16),
        )
        m = mods[l]
        mod_lat = [m[:bsz, i * d:(i + 1) * d][:, None, :] for i in range(6)]
        mod_ctx = [jnp.broadcast_to(m[bsz:bsz + 1, i * d:(i + 1) * d][:, None, :], (bsz, 1, d)) for i in range(6)]

        puv_c, qkt_c = _project(x_ctx, lw, mod_ctx, tl_ctx)
        puv, qkt = _project(x_lat, lw, mod_lat, tl_lat)
        seg_norm = lambda g: jnp.sqrt(QK_NOPE * jnp.max(g[:QK_NOPE] ** 2) + QK_ROPE * jnp.max(g[QK_NOPE:] ** 2))
        bound = QK_EXP2_SCALE * seg_norm(q_norm_g[l]) * seg_norm(k_norm_g[l])
        qpad = jnp.zeros((HEAD_PAD - QK_HEAD, KEY_TILE), F32).at[0].set(-bound)
        qt, k, vt = _qkv(qkt, qkt_c, bro(qa_norm_g[l]), bro(kva_norm_g[l]), bro(q_norm_g[l]), bro(k_norm_g[l]),
                         cos_t, sin_t, qpad, lw["w_uq_t"], lw["w_ukv_t"])
        lat_attn = functools.partial(_attention, qt, k, vt, 0, n_lat // tl_lat["attn"], tl_lat["attn"], 0, nl + nc)
        yat = lax.cond(2.0 * bound < EXP2_SAFE_RANGE, lambda: lat_attn(False), lambda: lat_attn(True))
        streams = [_mixer(x_lat, puv, yat, lw, mod_lat, rw, tl_lat) + (mod_lat[5], n_lat)]
        if not last:
            yat_c = _attention(qt, k, vt, nl, nc, KEY_TILE, nl, nc, True)
            streams.append(_mixer(x_ctx, puv_c, yat_c, lw, mod_ctx, rw, tl_ctx) + (mod_ctx[5], n_ctx))
        outs = _moe_block(streams, lw, rw, router_b)
        x_lat = outs[0].reshape(bsz, n_lat, d)
        if not last:
            x_ctx = outs[1].reshape(bsz, n_ctx, d)
    return x_lat
```

```python
import functools
import math

import jax
import jax.numpy as jnp
from jax import lax
from jax.experimental import pallas as pl
from jax.experimental.pallas import tpu as pltpu

F32 = jnp.float32
BF16 = jnp.bfloat16

GRID_W = 64
POOL_WINDOWS = (2, 4, 8, 16)
POOL_WIDTH = 256
POOL_GROUP = POOL_WIDTH // len(POOL_WINDOWS)
SGU_WIDTH = 256
SGU_GROUPS = 4
SGU_GROUP = SGU_WIDTH // SGU_GROUPS
CHUNK = 128
HEADS = 8
QK_NOPE = 64
QK_ROPE = 32
V_HEAD = 64
Q_LORA = 384
KV_LORA = 256
QK_HEAD = QK_NOPE + QK_ROPE
MLA_WIDTH = HEADS * V_HEAD
SM_SCALE = QK_HEAD ** -0.5
ROPE_BASE = 10000.0
ROPE_AXIS = QK_ROPE // 2
ROPE_PAIRS = ROPE_AXIS // 2
N_BRANCH = 3
OFF_U = POOL_WIDTH
OFF_V = OFF_U + SGU_WIDTH
OFF_QA = OFF_V + SGU_WIDTH
OFF_KVA = OFF_QA + Q_LORA
OFF_KR = OFF_KVA + KV_LORA
OFF_GATE = OFF_KR + QK_ROPE
N_EXPERTS = 16
N_GROUPS = 4
EXPERTS_PER_GROUP = N_EXPERTS // N_GROUPS
PAIRS_PER_GROUP = EXPERTS_PER_GROUP * (EXPERTS_PER_GROUP - 1) // 2
N_CLASSES = N_GROUPS * PAIRS_PER_GROUP
EPS = 1e-6

PUV_WIDTH = OFF_QA
QKT_ROWS = OFF_GATE - OFF_QA
HEAD_PAD = 128
QK_EXP2_SCALE = SM_SCALE * math.log2(math.e)
EXP2_SAFE_RANGE = 100.0

V7X_LANES = 128
V7X_SUBLANES = 8
V7X_VMEM_LIMIT = 52 * 1024 * 1024
KEY_TILE = 256
MOE_TILE = 256
ROW_TILE = (V7X_SUBLANES, V7X_LANES)


def _cparams(sem, vmem=V7X_VMEM_LIMIT):
    return pltpu.CompilerParams(dimension_semantics=sem, vmem_limit_bytes=vmem)


def _norm_mod(x, g, shift, scale):
    ms = jnp.mean(x * x, axis=-1, keepdims=True)
    y = x * lax.rsqrt(ms + EPS) * g
    return y * (1.0 + scale) + shift


def _rms_rows(x, g):
    ms = jnp.mean(x * x, axis=0, keepdims=True)
    return x * lax.rsqrt(ms + EPS) * g


def _adaln_kernel(c_ref, w_ref, b_ref, o_ref):
    c = c_ref[...]
    s = (c * jax.nn.sigmoid(c)).astype(BF16)
    o_ref[0] = jnp.dot(s, w_ref[0].astype(BF16), preferred_element_type=F32) + b_ref[0]


def _adaln(cond8, w_mod, b_mod):
    depth, d, n6 = w_mod.shape
    tn = 1536
    return pl.pallas_call(
        _adaln_kernel,
        out_shape=jax.ShapeDtypeStruct((depth, V7X_SUBLANES, n6), F32),
        grid=(depth, n6 // tn),
        in_specs=[
            pl.BlockSpec((V7X_SUBLANES, d), lambda l, j: (0, 0)),
            pl.BlockSpec((1, d, tn), lambda l, j: (l, 0, j)),
            pl.BlockSpec((1, 1, tn), lambda l, j: (l, 0, j)),
        ],
        out_specs=pl.BlockSpec((1, V7X_SUBLANES, tn), lambda l, j: (l, 0, j)),
        compiler_params=_cparams(("parallel", "parallel")),
        name="adaln",
    )(cond8, w_mod, b_mod.reshape(depth, 1, n6))


def _inproj_kernel(x_ref, g_ref, sh_ref, sc_ref, wn_ref, wt_ref, puv_ref, qkt_ref):
    h = _norm_mod(x_ref[0], g_ref[...], sh_ref[0], sc_ref[0]).astype(BF16)
    puv_ref[0] = jnp.dot(h, wn_ref[...], preferred_element_type=F32)
    qkt_ref[0] = lax.dot_general(wt_ref[...], h, (((1,), (1,)), ((), ())),
                                 preferred_element_type=F32)


def _inproj(x, g, shift, scale, w_nat, w_t, tm):
    b, n, d = x.shape
    return pl.pallas_call(
        _inproj_kernel,
        out_shape=(jax.ShapeDtypeStruct((b, n, PUV_WIDTH), F32),
                   jax.ShapeDtypeStruct((b, QKT_ROWS, n), F32)),
        grid=(b, n // tm),
        in_specs=[
            pl.BlockSpec((1, tm, d), lambda bi, i: (bi, i, 0)),
            pl.BlockSpec((1, d), lambda bi, i: (0, 0)),
            pl.BlockSpec((1, 1, d), lambda bi, i: (bi, 0, 0)),
            pl.BlockSpec((1, 1, d), lambda bi, i: (bi, 0, 0)),
            pl.BlockSpec((d, PUV_WIDTH), lambda bi, i: (0, 0)),
            pl.BlockSpec((QKT_ROWS, d), lambda bi, i: (0, 0)),
        ],
        out_specs=(pl.BlockSpec((1, tm, PUV_WIDTH), lambda bi, i: (bi, i, 0)),
                   pl.BlockSpec((1, QKT_ROWS, tm), lambda bi, i: (bi, 0, i))),
        compiler_params=_cparams(("parallel", "parallel")),
        name="inproj",
    )(x, g, shift, scale, w_nat, w_t)


POOL_HALO = 8


def _poolsgu_kernel(puv_ref, prev_ref, next_ref, poolw_ref, pscale_ref, sgug_ref, ws_ref,
                    bias_ref, o_ref, *, n, tc):
    i = pl.program_id(1)
    last = pl.num_programs(1) - 1
    puv = puv_ref[0]
    p = puv[:, :POOL_WIDTH]
    prev = jnp.where(i > 0, prev_ref[0], 0.0)
    nxt = jnp.where(i < last, next_ref[0], 0.0)
    e = jnp.concatenate([prev, p, nxt], axis=0)
    rows = tc + 2 * POOL_HALO
    a2 = e + pltpu.roll(e, 1, 0)
    a4 = a2 + pltpu.roll(a2, 2, 0)
    a8 = a4 + pltpu.roll(a4, 4, 0)
    a16 = a8 + pltpu.roll(a8, 8, 0)
    w2 = a2[POOL_HALO:POOL_HALO + tc]
    w4 = pltpu.roll(a4, rows - 1, 0)[POOL_HALO:POOL_HALO + tc]
    w8 = pltpu.roll(a8, rows - 3, 0)[POOL_HALO:POOL_HALO + tc]
    w16 = pltpu.roll(a16, rows - 7, 0)[POOL_HALO:POOL_HALO + tc]
    lane = lax.broadcasted_iota(jnp.int32, (tc, POOL_WIDTH), 1)
    t = i * tc + lax.broadcasted_iota(jnp.int32, (tc, POOL_WIDTH), 0)
    grp = lane // POOL_GROUP
    half = jnp.where(grp == 0, 1, jnp.where(grp == 1, 2, jnp.where(grp == 2, 4, 8)))
    cnt = (jnp.minimum(t + half, n) - jnp.maximum(t - half, 0)).astype(F32)
    wsum = jnp.where(grp == 0, w2, jnp.where(grp == 1, w4, jnp.where(grp == 2, w8, w16)))
    dlt = (wsum / cnt - p).astype(BF16)
    y_pool = jnp.dot(dlt, poolw_ref[...], preferred_element_type=F32) * pscale_ref[...]
    o_ref[0, :, 0:POOL_WIDTH] = y_pool.astype(o_ref.dtype)

    u = jax.nn.gelu(puv[:, OFF_U:OFF_V])
    gv = jax.nn.gelu(puv[:, OFF_V:OFF_QA])
    vn = gv * lax.rsqrt(jnp.mean(gv * gv, axis=-1, keepdims=True) + EPS) * sgug_ref[...]
    clane = lax.broadcasted_iota(jnp.int32, (CHUNK, SGU_WIDTH), 1) // SGU_GROUP
    for c in range(tc // CHUNK):
        vc = vn[c * CHUNK:(c + 1) * CHUNK]
        mixed = bias_ref[...]
        for gi in range(SGU_GROUPS):
            vm = jnp.where(clane == gi, vc, 0.0).astype(BF16)
            mixed = mixed + jnp.dot(ws_ref[gi], vm, preferred_element_type=F32)
        y = u[c * CHUNK:(c + 1) * CHUNK] * mixed
        o_ref[0, c * CHUNK:(c + 1) * CHUNK, POOL_WIDTH:POOL_WIDTH + SGU_WIDTH] = y.astype(o_ref.dtype)


def _poolsgu(puv, poolw, pscale, sgug, ws, bias, tc):
    b, n, _ = puv.shape
    nh = n // POOL_HALO
    per = tc // POOL_HALO
    kern = functools.partial(_poolsgu_kernel, n=n, tc=tc)
    return pl.pallas_call(
        kern,
        out_shape=jax.ShapeDtypeStruct((b, n, POOL_WIDTH + SGU_WIDTH), BF16),
        grid=(b, n // tc),
        in_specs=[
            pl.BlockSpec((1, tc, PUV_WIDTH), lambda bi, i: (bi, i, 0)),
            pl.BlockSpec((1, POOL_HALO, POOL_WIDTH),
                         lambda bi, i: (bi, jnp.maximum(i * per - 1, 0), 0)),
            pl.BlockSpec((1, POOL_HALO, POOL_WIDTH),
                         lambda bi, i: (bi, jnp.minimum((i + 1) * per, nh - 1), 0)),
            pl.BlockSpec((POOL_WIDTH, POOL_WIDTH), lambda bi, i: (0, 0)),
            pl.BlockSpec((1, POOL_WIDTH), lambda bi, i: (0, 0)),
            pl.BlockSpec((1, SGU_WIDTH), lambda bi, i: (0, 0)),
            pl.BlockSpec((SGU_GROUPS, CHUNK, CHUNK), lambda bi, i: (0, 0, 0)),
            pl.BlockSpec((CHUNK, SGU_WIDTH), lambda bi, i: (0, 0)),
        ],
        out_specs=pl.BlockSpec((1, tc, POOL_WIDTH + SGU_WIDTH), lambda bi, i: (bi, i, 0)),
        compiler_params=_cparams(("parallel", "parallel")),
        name="poolsgu",
    )(puv, puv, puv, poolw, pscale, sgug, ws, bias)


V_ROWS = 80


def _qkv_kernel(lat_ref, ctx_ref, gqa_ref, gkva_ref, gq_ref, gk_ref, cos_ref, sin_ref, qpad_ref, wq_ref, wkv_ref,
                qt_ref, k_ref, vt_ref, *, n_lat_tiles):
    tm = KEY_TILE
    blk = jnp.where(pl.program_id(1) < n_lat_tiles, lat_ref[0], ctx_ref[0])
    qa_n = _rms_rows(blk[0:Q_LORA], gqa_ref[...]).astype(BF16)
    kva_n = _rms_rows(blk[Q_LORA:Q_LORA + KV_LORA], gkva_ref[...]).astype(BF16)
    q = jnp.dot(wq_ref[...], qa_n, preferred_element_type=F32)
    kv = jnp.dot(wkv_ref[...], kva_n, preferred_element_type=F32)
    gq = gq_ref[...]
    gk = gk_ref[...]

    def rope(x):
        r = ROPE_PAIRS
        rot = jnp.concatenate([-x[r:2 * r], x[0:r], -x[3 * r:4 * r], x[2 * r:3 * r]], axis=0)
        return x * cos_ref[...] + rot * sin_ref[...]

    k_rope = rope(_rms_rows(blk[Q_LORA + KV_LORA:QKT_ROWS], gk[QK_NOPE:QK_HEAD]))
    first_pad = lax.broadcasted_iota(jnp.int32, (HEAD_PAD - QK_HEAD, tm), 0) == 0
    kpad = jnp.where(first_pad, 1.0, 0.0)
    qpad = qpad_ref[...]
    ones_rows = jnp.where(lax.broadcasted_iota(jnp.int32, (V_ROWS - V_HEAD, tm), 0) == 0, 1.0, 0.0)
    kv_rows = QK_NOPE + V_HEAD
    for h in range(HEADS):
        qh = q[h * QK_HEAD:(h + 1) * QK_HEAD]
        q_nope = _rms_rows(qh[0:QK_NOPE], gq[0:QK_NOPE])
        q_rope = rope(_rms_rows(qh[QK_NOPE:QK_HEAD], gq[QK_NOPE:QK_HEAD]))
        qcat = jnp.concatenate([q_nope * QK_EXP2_SCALE, q_rope * QK_EXP2_SCALE, qpad], axis=0)
        qt_ref[0, h] = qcat.astype(qt_ref.dtype)
        k_nope = _rms_rows(kv[h * kv_rows:h * kv_rows + QK_NOPE], gk[0:QK_NOPE])
        kcat = jnp.concatenate([k_nope, k_rope, kpad], axis=0)
        k_ref[0, h, 0] = kcat.T.astype(k_ref.dtype)
        v = jnp.concatenate([kv[h * kv_rows + QK_NOPE:(h + 1) * kv_rows], ones_rows], axis=0)
        vt_ref[0, h, 0] = v.astype(vt_ref.dtype)


def _qkv(qkt_lat, qkt_ctx, gqa, gkva, gq, gk, cos_t, sin_t, qpad, wq_t, wkv_t):
    b, _, n_lat = qkt_lat.shape
    n_ctx = qkt_ctx.shape[2]
    tm = KEY_TILE
    nl, nc = n_lat // tm, n_ctx // tm
    nt = nl + nc
    kern = functools.partial(_qkv_kernel, n_lat_tiles=nl)
    full = lambda shape: pl.BlockSpec(shape, lambda bi, i: (0,) * len(shape))
    return pl.pallas_call(
        kern,
        out_shape=(jax.ShapeDtypeStruct((b, HEADS, HEAD_PAD, nt * tm), BF16),
                   jax.ShapeDtypeStruct((b, HEADS, nt, tm, HEAD_PAD), BF16),
                   jax.ShapeDtypeStruct((b, HEADS, nt, V_ROWS, tm), BF16)),
        grid=(b, nt),
        in_specs=[
            pl.BlockSpec((1, QKT_ROWS, tm), lambda bi, i: (bi, 0, jnp.minimum(i, nl - 1))),
            pl.BlockSpec((1, QKT_ROWS, tm), lambda bi, i: (bi, 0, jnp.maximum(i - nl, 0))),
            full((Q_LORA, tm)), full((KV_LORA, tm)), full((QK_HEAD, tm)), full((QK_HEAD, tm)),
            pl.BlockSpec((QK_ROPE, tm), lambda bi, i: (0, i)),
            pl.BlockSpec((QK_ROPE, tm), lambda bi, i: (0, i)),
            full((HEAD_PAD - QK_HEAD, tm)),
            full((HEADS * QK_HEAD, Q_LORA)), full((HEADS * (QK_NOPE + V_HEAD), KV_LORA)),
        ],
        out_specs=(pl.BlockSpec((1, HEADS, HEAD_PAD, tm), lambda bi, i: (bi, 0, 0, i)),
                   pl.BlockSpec((1, HEADS, 1, tm, HEAD_PAD), lambda bi, i: (bi, 0, i, 0, 0)),
                   pl.BlockSpec((1, HEADS, 1, V_ROWS, tm), lambda bi, i: (bi, 0, i, 0, 0))),
        compiler_params=_cparams(("parallel", "parallel")),
        name="qkv",
    )(qkt_lat, qkt_ctx, gqa, gkva, gq, gk, cos_t, sin_t, qpad, wq_t, wkv_t)


def _attn_kernel(qt_ref, k_ref, vt_ref, o_ref, m_ref, acc_ref, s_ref, *, g_n, s_n, per, online):
    if online:
        m_ref[...] = jnp.full(m_ref.shape, -jnp.inf, F32)
    acc_ref[...] = jnp.zeros(acc_ref.shape, F32)

    def qk(g, s, j, slot):
        s_ref[slot, g * s_n + s] = jnp.dot(k_ref[0, g, s * per + j], qt_ref[0, g],
                                            preferred_element_type=F32)

    def soft_pv(g, s, j, slot):
        ci = g * s_n + s
        sc = s_ref[slot, ci]
        vt = vt_ref[0, g, s * per + j]
        if online:
            m = m_ref[ci]
            m_new = jnp.maximum(m, jnp.max(sc, axis=0, keepdims=True))
            p = jnp.exp2(sc - m_new).astype(BF16)
            acc_ref[ci] = jnp.exp2(m - m_new) * acc_ref[ci] + jnp.dot(vt, p, preferred_element_type=F32)
            m_ref[ci] = m_new
        else:
            acc_ref[ci] += jnp.dot(vt, jnp.exp2(sc).astype(BF16), preferred_element_type=F32)

    for g in range(g_n):
        for s in range(s_n):
            qk(g, s, 0, 0)

    def body(jj, c):
        for half in range(2):
            j = 2 * jj + half
            for g in range(g_n):
                for s in range(s_n):
                    qk(g, s, j + 1, 1 - half)
                    soft_pv(g, s, j, half)
        return c

    if per > 1:
        lax.fori_loop(0, (per - 1) // 2, body, 0)
    for g in range(g_n):
        for s in range(s_n):
            soft_pv(g, s, per - 1, (per - 1) % 2)
    for g in range(g_n):
        tot = None
        if online:
            m = m_ref[g * s_n]
            for s in range(1, s_n):
                m = jnp.maximum(m, m_ref[g * s_n + s])
        for s in range(s_n):
            part = acc_ref[g * s_n + s]
            if online:
                part = part * jnp.exp2(m_ref[g * s_n + s] - m)
            tot = part if tot is None else tot + part
        o_ref[0, g * V_HEAD:(g + 1) * V_HEAD] = (tot[0:V_HEAD] / tot[V_HEAD:V_HEAD + 1]).astype(o_ref.dtype)


ATTN_HEADS_PER_STEP = 2


def _attention(qt, k, vt, q0, nq, tq, key0, nkeys, online):
    b = qt.shape[0]
    g_n = ATTN_HEADS_PER_STEP
    s_n = next(s for s in (11, 3, 1) if nkeys % s == 0 and (nkeys // s - 1) % 2 == 0)
    assert key0 % nkeys == 0 and HEADS % g_n == 0
    kb = key0 // nkeys
    kern = functools.partial(_attn_kernel, g_n=g_n, s_n=s_n, per=nkeys // s_n, online=online)
    return pl.pallas_call(
        kern,
        out_shape=jax.ShapeDtypeStruct((b, MLA_WIDTH, nq * tq), BF16),
        grid=(b, HEADS // g_n, nq),
        in_specs=[
            pl.BlockSpec((1, g_n, HEAD_PAD, tq), lambda bi, hg, i: (bi, hg, 0, q0 + i)),
            pl.BlockSpec((1, g_n, nkeys, KEY_TILE, HEAD_PAD), lambda bi, hg, i: (bi, hg, kb, 0, 0)),
            pl.BlockSpec((1, g_n, nkeys, V_ROWS, KEY_TILE), lambda bi, hg, i: (bi, hg, kb, 0, 0)),
        ],
        out_specs=pl.BlockSpec((1, g_n * V_HEAD, tq), lambda bi, hg, i: (bi, hg, i)),
        scratch_shapes=[pltpu.VMEM((g_n * s_n, 1, tq), F32),
                        pltpu.VMEM((g_n * s_n, V_ROWS, tq), F32),
                        pltpu.VMEM((2, g_n * s_n, KEY_TILE, tq), F32)],
        compiler_params=_cparams(("parallel", "parallel", "parallel")),
        name="attention",
    )(qt, k, vt)


def _router_logits(h2, rw):
    hi = h2.astype(BF16)
    lo = (h2 - hi.astype(F32)).astype(BF16)
    r_hi = jnp.dot(hi, rw, preferred_element_type=F32)
    r_lo = jnp.dot(lo, rw, preferred_element_type=F32)
    lg = r_hi + pltpu.roll(r_hi, V7X_LANES - N_EXPERTS, 1) + r_lo
    lane = lax.broadcasted_iota(jnp.int32, lg.shape, 1)
    return jnp.where(lane < N_EXPERTS, lg, 0.0)


def _merge_kernel(x_ref, g1n_ref, sh1_ref, sc1_ref, gt1_ref, wg_ref, bg_ref, yps_ref, yat_ref,
                  wbp_ref, wbs_ref, wbm_ref, wout_ref, g2n_ref, sh2_ref, sc2_ref, rw_ref,
                  xmid_ref, h2_ref, lg_ref):
    x = x_ref[0]
    d = x.shape[1]
    h = _norm_mod(x, g1n_ref[...], sh1_ref[0], sc1_ref[0]).astype(BF16)
    yps = yps_ref[0]
    branches = (
        jnp.dot(yps[:, 0:POOL_WIDTH], wbp_ref[...], preferred_element_type=F32),
        jnp.dot(yps[:, POOL_WIDTH:POOL_WIDTH + SGU_WIDTH], wbs_ref[...], preferred_element_type=F32),
        lax.dot_general(yat_ref[0], wbm_ref[...], (((0,), (0,)), ((), ())),
                        preferred_element_type=F32),
    )
    merged = None
    for i in range(N_BRANCH):
        gl = jnp.dot(h, wg_ref[:, i * d:(i + 1) * d], preferred_element_type=F32) + bg_ref[:, i * d:(i + 1) * d]
        term = jax.nn.sigmoid(gl) * branches[i]
        merged = term if merged is None else merged + term
    out = jnp.dot(merged.astype(BF16), wout_ref[...], preferred_element_type=F32)
    xm = x + gt1_ref[0] * out
    xmid_ref[0] = xm
    h2 = _norm_mod(xm, g2n_ref[...], sh2_ref[0], sc2_ref[0])
    h2_ref[0] = h2.reshape((h2.shape[0],) + ROW_TILE)
    lg_ref[0] = _router_logits(h2, rw_ref[...])


def _merge(x, g1n, sh1, sc1, gt1, wg, bg, yps, yat, wbp, wbs, wbm, wout, g2n, sh2, sc2, rw, tm):
    b, n, d = x.shape
    row = lambda: pl.BlockSpec((1, 1, d), lambda bi, i: (bi, 0, 0))
    full = lambda shape: pl.BlockSpec(shape, lambda bi, i: (0,) * len(shape))
    return pl.pallas_call(
        _merge_kernel,
        out_shape=(jax.ShapeDtypeStruct((b, n, d), F32),
                   jax.ShapeDtypeStruct((b, n) + ROW_TILE, F32),
                   jax.ShapeDtypeStruct((b, n, V7X_LANES), F32)),
        grid=(b, n // tm),
        in_specs=[
            pl.BlockSpec((1, tm, d), lambda bi, i: (bi, i, 0)),
            full((1, d)), row(), row(), row(),
            full((d, N_BRANCH * d)), full((1, N_BRANCH * d)),
            pl.BlockSpec((1, tm, POOL_WIDTH + SGU_WIDTH), lambda bi, i: (bi, i, 0)),
            pl.BlockSpec((1, MLA_WIDTH, tm), lambda bi, i: (bi, 0, i)),
            full((POOL_WIDTH, d)), full((SGU_WIDTH, d)), full((MLA_WIDTH, d)), full((d, d)),
            full((1, d)), row(), row(),
            full((d, V7X_LANES)),
        ],
        out_specs=(pl.BlockSpec((1, tm, d), lambda bi, i: (bi, i, 0)),
                   pl.BlockSpec((1, tm) + ROW_TILE, lambda bi, i: (bi, i, 0, 0)),
                   pl.BlockSpec((1, tm, V7X_LANES), lambda bi, i: (bi, i, 0))),
        compiler_params=_cparams(("parallel", "parallel")),
        name="merge",
    )(x, g1n, sh1, sc1, gt1, wg, bg, yps, yat, wbp, wbs, wbm, wout, g2n, sh2, sc2, rw)


CLS_ROWS = 32


def _route_kernel(ext_ref, rb_ref, tri_ref, init_ref, cls_ref, rank_ref, cnt_ref, carry_ref):
    tn = ext_ref.shape[0]

    @pl.when(pl.program_id(0) == 0)
    def _():
        carry_ref[...] = jnp.broadcast_to(init_ref[:, 0:1], carry_ref.shape)

    scores = jax.nn.sigmoid(ext_ref[...].T[0:N_EXPERTS])
    sel = scores + rb_ref[...]
    best = None
    for g in range(N_GROUPS):
        rows = [sel[g * EXPERTS_PER_GROUP + j:g * EXPERTS_PER_GROUP + j + 1] for j in range(EXPERTS_PER_GROUP)]
        v1, i1 = rows[0], jnp.zeros_like(rows[0], jnp.int32)
        for j in range(1, EXPERTS_PER_GROUP):
            gt = rows[j] > v1
            v1 = jnp.where(gt, rows[j], v1)
            i1 = jnp.where(gt, j, i1)
        v2 = jnp.full_like(v1, -jnp.inf)
        i2 = jnp.zeros_like(i1)
        for j in range(EXPERTS_PER_GROUP):
            gt = (i1 != j) & (rows[j] > v2)
            v2 = jnp.where(gt, rows[j], v2)
            i2 = jnp.where(gt, j, i2)
        gs = v1 + v2
        if best is None:
            best = (gs, jnp.full_like(i1, g), i1, i2)
        else:
            gt = gs > best[0]
            cand = (gs, jnp.full_like(i1, g), i1, i2)
            best = tuple(jnp.where(gt, c, b) for c, b in zip(cand, best))
    _, grp, i1, i2 = best
    lo = jnp.minimum(i1, i2)
    hi = jnp.maximum(i1, i2)
    base = jnp.where(lo == 0, 0, jnp.where(lo == 1, 3, 5))
    cls = grp * PAIRS_PER_GROUP + base + (hi - lo - 1)
    cls_ref[...] = cls
    onehot = jnp.where(lax.broadcasted_iota(jnp.int32, (CLS_ROWS, tn), 0) == cls, 1.0, 0.0)
    cum = jnp.dot(onehot.astype(BF16), tri_ref[...], preferred_element_type=F32)
    carry = carry_ref[...]
    rank_ref[...] = (jnp.sum(onehot * (carry + cum), axis=0, keepdims=True) - 1.0).astype(jnp.int32)
    carry = carry + jnp.broadcast_to(cum[:, tn - 1:tn], carry.shape)
    carry_ref[...] = carry
    cnt_ref[...] = carry[:, 0:V7X_LANES]


def _route(logits, rb, init_counts):
    t = logits.shape[0]
    tn = next(w for w in (512, 256) if t % w == 0)
    tri = (jnp.arange(tn)[:, None] <= jnp.arange(tn)[None, :]).astype(BF16)
    return pl.pallas_call(
        _route_kernel,
        out_shape=(jax.ShapeDtypeStruct((1, t), jnp.int32), jax.ShapeDtypeStruct((1, t), jnp.int32),
                   jax.ShapeDtypeStruct((CLS_ROWS, V7X_LANES), F32)),
        grid=(t // tn,),
        in_specs=[pl.BlockSpec((tn, V7X_LANES), lambda i: (i, 0)),
                  pl.BlockSpec((N_EXPERTS, tn), lambda i: (0, 0)),
                  pl.BlockSpec((tn, tn), lambda i: (0, 0)),
                  pl.BlockSpec((CLS_ROWS, V7X_LANES), lambda i: (0, 0))],
        out_specs=(pl.BlockSpec((1, tn), lambda i: (0, i)), pl.BlockSpec((1, tn), lambda i: (0, i)),
                   pl.BlockSpec((CLS_ROWS, V7X_LANES), lambda i: (0, 0))),
        scratch_shapes=[pltpu.VMEM((CLS_ROWS, tn), F32)],
        compiler_params=_cparams(("arbitrary",)),
        name="route",
    )(logits, jnp.broadcast_to(rb[:, None], (N_EXPERTS, tn)), tri, init_counts)


def _plan(cls, rank, counts, tmo):
    t = cls.shape[0]
    n_tiles = t // tmo + N_CLASSES
    class_ids = jnp.arange(N_CLASSES, dtype=jnp.int32)
    tiles = (counts + tmo - 1) // tmo
    tile_end = jnp.cumsum(tiles)
    tile_start = tile_end - tiles
    n_used = tile_end[-1]
    pos = rank + jnp.sum(jnp.where(cls[None, :] == class_ids[:, None], (tile_start * tmo)[:, None], 0), axis=0)
    tile_id = jnp.arange(n_tiles, dtype=jnp.int32)
    tile_cls = jnp.sum(tile_id[:, None] >= tile_end[None, :], axis=1).astype(jnp.int32)
    last_cls = jnp.max(jnp.where(counts > 0, class_ids, 0))
    tile_cls = jnp.where(tile_id < n_used, tile_cls, last_cls)
    grp = tile_cls // PAIRS_PER_GROUP
    pair = tile_cls % PAIRS_PER_GROUP
    lo = jnp.where(pair < 3, 0, jnp.where(pair < 5, 1, 2))
    hi = jnp.where(pair < 3, pair + 1, jnp.where(pair < 5, pair - 1, 3))
    tile_lo = (grp * EXPERTS_PER_GROUP + lo).astype(jnp.int32)
    tile_hi = (grp * EXPERTS_PER_GROUP + hi).astype(jnp.int32)
    pad_start = (tile_start * tmo + counts).astype(jnp.int32)
    pad_len = (tiles * tmo - counts).astype(jnp.int32)
    return (pos.astype(jnp.int32), tile_lo, tile_hi, n_used.reshape(1).astype(jnp.int32), pad_start, pad_len)


GATHER_UNROLL = 8
DMA_PRIORITIES = 2
PAD_BITS = 8


def _dispatch_kernel(pos_ref, pstart_ref, plen_ref, nused_ref, *refs, src_steps, rs):
    n_src = len(src_steps)
    srcs, out, zbuf, sem = refs[:n_src], refs[n_src], refs[n_src + 1], refs[n_src + 2]
    i = pl.program_id(0)
    last = pl.num_programs(0) - 1

    def pad_copies(act):
        for c in range(N_CLASSES):
            ln, st = plen_ref[c], pstart_ref[c]
            for k in range(PAD_BITS):
                size = 1 << k

                @pl.when((ln & size) != 0)
                def _(size=size, ln=ln, st=st):
                    off = st + (ln & (size - 1))
                    act(pltpu.make_async_copy(zbuf.at[pl.ds(0, size)], out.at[pl.ds(off, size)], sem.at[1]))
        half = zbuf.shape[0]
        n_tiles = out.shape[0] // (2 * half)
        for j in range(n_tiles - N_CLASSES, n_tiles):
            @pl.when(j >= nused_ref[0])
            def _(j=j):
                for part in range(2):
                    act(pltpu.make_async_copy(zbuf, out.at[pl.ds((2 * j + part) * half, half)], sem.at[1]))

    @pl.when(i == 0)
    def _():
        zbuf[...] = jnp.zeros_like(zbuf)
        pad_copies(lambda cp: cp.start())

    first = 0
    for src, steps in zip(srcs, src_steps):
        @pl.when((i >= first) & (i < first + steps))
        def _(src=src):
            def body(rr, c):
                for k in range(DMA_PRIORITIES):
                    r = DMA_PRIORITIES * rr + k
                    pltpu.make_async_copy(src.at[pl.ds(r, 1)], out.at[pl.ds(pos_ref[i * rs + r], 1)],
                                          sem.at[0]).start(priority=k)
                return c

            lax.fori_loop(0, rs // DMA_PRIORITIES, body, 0, unroll=GATHER_UNROLL // DMA_PRIORITIES)
            pltpu.make_async_copy(src, out.at[pl.ds(0, rs)], sem.at[0]).wait()
        first += steps

    @pl.when(i == last)
    def _():
        pad_copies(lambda cp: cp.wait())


def _dispatch(pos, pad_start, pad_len, n_used, sources, n_rows):
    rs = next(w for w in (1024, 512, 256) if all(s.shape[0] % w == 0 for s in sources))
    src_steps = tuple(s.shape[0] // rs for s in sources)
    firsts = [sum(src_steps[:k]) for k in range(len(sources))]

    def block(first, steps):
        return pl.BlockSpec((rs,) + ROW_TILE, lambda i, *_: (jnp.clip(i - first, 0, steps - 1), 0, 0))

    grid_spec = pltpu.PrefetchScalarGridSpec(
        num_scalar_prefetch=4,
        grid=(sum(src_steps),),
        in_specs=[block(f, st) for f, st in zip(firsts, src_steps)],
        out_specs=pl.BlockSpec(memory_space=pl.ANY),
        scratch_shapes=[pltpu.VMEM((1 << (PAD_BITS - 1),) + ROW_TILE, F32), pltpu.SemaphoreType.DMA((2,))],
    )
    return pl.pallas_call(
        functools.partial(_dispatch_kernel, src_steps=src_steps, rs=rs),
        out_shape=jax.ShapeDtypeStruct((n_rows,) + ROW_TILE, F32),
        grid_spec=grid_spec,
        compiler_params=_cparams(("arbitrary",)),
        name="dispatch",
    )(pos, pad_start, pad_len, n_used, *sources)


def _moe_kernel(elo_ref, ehi_ref, nused_ref, xs_ref, rw_ref, wgl_ref, wul_ref, wdl_ref,
                wgh_ref, wuh_ref, wdh_ref, y_ref):
    i = pl.program_id(0)
    tmo = xs_ref.shape[0]

    @pl.when(i < nused_ref[0])
    def _():
        x = xs_ref[...].reshape(tmo, -1)
        xs = x.astype(BF16)
        scores = jax.nn.sigmoid(jnp.dot(xs, rw_ref[...], preferred_element_type=F32))
        lane = lax.broadcasted_iota(jnp.int32, scores.shape, 1)
        s_lo = jnp.sum(jnp.where(lane == elo_ref[i], scores, 0.0), axis=1, keepdims=True)
        s_hi = jnp.sum(jnp.where(lane == ehi_ref[i], scores, 0.0), axis=1, keepdims=True)
        tot = s_lo + s_hi

        def ffn(wg_ref, wu_ref, w_row):
            gate = jnp.dot(xs, wg_ref[0], preferred_element_type=F32)
            up = jnp.dot(xs, wu_ref[0], preferred_element_type=F32)
            return ((gate * jax.nn.sigmoid(gate)) * up * w_row).astype(BF16)

        y = jnp.dot(ffn(wgl_ref, wul_ref, s_lo / tot), wdl_ref[0], preferred_element_type=F32)
        y = y + jnp.dot(ffn(wgh_ref, wuh_ref, s_hi / tot), wdh_ref[0], preferred_element_type=F32)
        y_ref[...] = y.reshape(y_ref.shape)

    @pl.when(i >= nused_ref[0])
    def _():
        y_ref[...] = jnp.zeros_like(y_ref)


def _moe(tile_lo, tile_hi, n_used, x_sorted, rw, wg, wu, wd, tmo):
    n_tiles = x_sorted.shape[0] // tmo
    d, de = wg.shape[1], wg.shape[2]
    lo3 = lambda i, elo, ehi, nu: (elo[i], 0, 0)
    hi3 = lambda i, elo, ehi, nu: (ehi[i], 0, 0)
    grid_spec = pltpu.PrefetchScalarGridSpec(
        num_scalar_prefetch=3,
        grid=(n_tiles,),
        in_specs=[
            pl.BlockSpec((tmo,) + ROW_TILE, lambda i, elo, ehi, nu: (jnp.minimum(i, nu[0] - 1), 0, 0)),
            pl.BlockSpec((d, V7X_LANES), lambda i, elo, ehi, nu: (0, 0)),
            pl.BlockSpec((1, d, de), lo3), pl.BlockSpec((1, d, de), lo3), pl.BlockSpec((1, de, d), lo3),
            pl.BlockSpec((1, d, de), hi3), pl.BlockSpec((1, d, de), hi3), pl.BlockSpec((1, de, d), hi3),
        ],
        out_specs=pl.BlockSpec((tmo,) + ROW_TILE, lambda i, elo, ehi, nu: (i, 0, 0)),
    )
    return pl.pallas_call(
        _moe_kernel,
        out_shape=jax.ShapeDtypeStruct((n_tiles * tmo,) + ROW_TILE, F32),
        grid_spec=grid_spec,
        compiler_params=_cparams(("arbitrary",)),
        name="moe",
    )(tile_lo, tile_hi, n_used, x_sorted, rw, wg, wu, wd, wg, wu, wd)


def _start_row_gather(src_hbm, idx_ref, base, dst, sem, rows):
    def body(rr, c):
        for k in range(DMA_PRIORITIES):
            r = DMA_PRIORITIES * rr + k
            pltpu.make_async_copy(src_hbm.at[pl.ds(idx_ref[base + r], 1)], dst.at[pl.ds(r, 1)],
                                  sem).start(priority=k)
        return c

    lax.fori_loop(0, rows // DMA_PRIORITIES, body, 0, unroll=GATHER_UNROLL // DMA_PRIORITIES)


def _wait_row_gather(src_hbm, dst, sem, rows):
    pltpu.make_async_copy(src_hbm.at[pl.ds(0, rows)], dst, sem).wait()


def _combine_kernel(pos_ref, x_ref, g_ref, y_hbm, o_ref, ybuf, sem, *, tmc):
    i = pl.program_id(0)
    last = pl.num_programs(0) - 1

    @pl.when(i == 0)
    def _():
        _start_row_gather(y_hbm, pos_ref, 0, ybuf.at[0], sem.at[0], tmc)

    @pl.when(i < last)
    def _():
        nxt = (i + 1) % 2
        _start_row_gather(y_hbm, pos_ref, (i + 1) * tmc, ybuf.at[nxt], sem.at[nxt], tmc)

    slot = i % 2
    _wait_row_gather(y_hbm, ybuf.at[slot], sem.at[slot], tmc)
    o_ref[...] = x_ref[...] + g_ref[0] * ybuf[slot].reshape(o_ref.shape)


def _combine(pos, x2d, gate, y_sorted, n, tmc):
    t, d = x2d.shape
    per_batch = n // tmc
    grid_spec = pltpu.PrefetchScalarGridSpec(
        num_scalar_prefetch=1,
        grid=(t // tmc,),
        in_specs=[
            pl.BlockSpec((tmc, d), lambda i, pos: (i, 0)),
            pl.BlockSpec((1, 1, d), lambda i, pos: (i // per_batch, 0, 0)),
            pl.BlockSpec(memory_space=pl.ANY),
        ],
        out_specs=pl.BlockSpec((tmc, d), lambda i, pos: (i, 0)),
        scratch_shapes=[pltpu.VMEM((2, tmc) + ROW_TILE, F32), pltpu.SemaphoreType.DMA((2,))],
    )
    return pl.pallas_call(
        functools.partial(_combine_kernel, tmc=tmc),
        out_shape=jax.ShapeDtypeStruct((t, d), F32),
        grid_spec=grid_spec,
        compiler_params=_cparams(("arbitrary",)),
        name="combine",
    )(pos, x2d, gate, y_sorted)


def _tiles(n):
    big = 512 if n % 512 == 0 else 256
    return dict(inproj=big, poolsgu=big, attn=big, merge=big)


def _project(x, lw, mod, tl):
    return _inproj(x, lw["norm1_g"], mod[0], mod[1], lw["w_puv"], lw["w_qkt"], tl["inproj"])


def _mixer(x, puv, yat, lw, mod, rw, tl):
    b, n, d = x.shape
    sh1, sc1, gt1, sh2, sc2, _ = mod
    yps = _poolsgu(puv, lw["pool_bd"], lw["pool_scale"], lw["sgu_norm_g"], lw["sgu_ws"], lw["sgu_bias"],
                   tl["poolsgu"])
    xmid, h2r, lg = _merge(x, lw["norm1_g"], sh1, sc1, gt1, lw["w_gate"], lw["b_gate"], yps, yat,
                           lw["w_br_pool"], lw["w_br_sgu"], lw["w_br_mla"], lw["w_out"],
                           lw["norm2_g"], sh2, sc2, rw, tl["merge"])
    return xmid.reshape(b * n, d), h2r.reshape((b * n,) + ROW_TILE), lg.reshape(b * n, V7X_LANES)


def _moe_block(streams, lw, rw, rb):
    counts = jnp.zeros((CLS_ROWS, V7X_LANES), F32)
    cls_l, rank_l = [], []
    for st in streams:
        cls, rank, counts = _route(st[2], rb, counts)
        cls_l.append(cls[0])
        rank_l.append(rank[0])
    t_all = sum(c.shape[0] for c in cls_l)
    pos, tile_lo, tile_hi, n_used, pad_start, pad_len = _plan(
        jnp.concatenate(cls_l), jnp.concatenate(rank_l), counts[:N_CLASSES, 0].astype(jnp.int32), MOE_TILE)
    n_rows = (t_all // MOE_TILE + N_CLASSES) * MOE_TILE
    x_sorted = _dispatch(pos, pad_start, pad_len, n_used, [st[1] for st in streams], n_rows)
    y_sorted = _moe(tile_lo, tile_hi, n_used, x_sorted, rw, lw["w_e_gate"], lw["w_e_up"], lw["w_e_down"], MOE_TILE)
    outs, first = [], 0
    for xmid, _, _, gate, n in streams:
        t = xmid.shape[0]
        outs.append(_combine(pos[first:first + t], xmid, gate, y_sorted, n, MOE_TILE))
        first += t
    return outs


def _rope_tables_t(n):
    rows = n // GRID_W
    row = jnp.repeat(jnp.arange(rows, dtype=F32), GRID_W)
    col = jnp.tile(jnp.arange(GRID_W, dtype=F32), rows)
    inv = ROPE_BASE ** (-(jnp.arange(ROPE_PAIRS, dtype=F32) * 2.0 / ROPE_AXIS))
    ang_r = inv[:, None] * row[None, :]
    ang_c = inv[:, None] * col[None, :]
    ang = jnp.concatenate([ang_r, ang_r, ang_c, ang_c], axis=0)
    return jnp.cos(ang), jnp.sin(ang)


def kernel(x, c, ctx, c_ctx, w_mod, b_mod, norm1_g, norm2_g, w_in, pool_w, pool_scale, sgu_norm_g, sgu_ws, sgu_b, qa_norm_g, w_uq, kva_norm_g, w_ukv, q_norm_g, k_norm_g, w_br_pool, w_br_sgu, w_br_mla, b_gate, w_out, router_w, router_b, w_e_gate, w_e_up, w_e_down):
    bsz, n_lat, d = x.shape
    n_ctx = ctx.shape[1]
    depth = w_mod.shape[0]
    assert bsz + 1 <= V7X_SUBLANES

    cond8 = jnp.zeros((V7X_SUBLANES, d), F32).at[:bsz].set(c).at[bsz].set(c_ctx)
    mods = _adaln(cond8, w_mod, b_mod)
    cos_l, sin_l = _rope_tables_t(n_lat)
    cos_t = jnp.concatenate([cos_l, jnp.ones((QK_ROPE, n_ctx), F32)], axis=1)
    sin_t = jnp.concatenate([sin_l, jnp.zeros((QK_ROPE, n_ctx), F32)], axis=1)
    nl, nc = n_lat // KEY_TILE, n_ctx // KEY_TILE
    bro = lambda g: jnp.broadcast_to(g[:, None], (g.shape[0], KEY_TILE))
    rw_hi = router_w.astype(BF16)
    rw_lo = (router_w - rw_hi.astype(F32)).astype(BF16)
    rw = jnp.concatenate([rw_hi, rw_lo, jnp.zeros((d, V7X_LANES - 2 * N_EXPERTS), BF16)], axis=1)
    tl_lat, tl_ctx = _tiles(n_lat), _tiles(n_ctx)

    x_lat, x_ctx = x, ctx
    for l in range(depth):
        last = l == depth - 1
        wl = w_in[l]
        lw = dict(
            norm1_g=norm1_g[l][None], norm2_g=norm2_g[l][None],
            w_puv=wl[:, :PUV_WIDTH].astype(BF16),
            w_qkt=wl[:, OFF_QA:OFF_GATE].T.astype(BF16),
            w_gate=wl[:, OFF_GATE:].astype(BF16), b_gate=b_gate[l][None],
            pool_bd=jax.scipy.linalg.block_diag(*[pool_w[l, g] for g in range(len(POOL_WINDOWS))]).astype(BF16),
            pool_scale=pool_scale[l][None], sgu_norm_g=sgu_norm_g[l][None],
            sgu_ws=sgu_ws[l].astype(BF16), sgu_bias=jnp.repeat(sgu_b[l].T, SGU_GROUP, axis=1),
            w_uq_t=w_uq[l].T.astype(BF16), w_ukv_t=w_ukv[l].T.astype(BF16),
            w_br_pool=w_br_pool[l].astype(BF16), w_br_sgu=w_br_sgu[l].astype(BF16),
            w_br_mla=w_br_mla[l].astype(BF16), w_out=w_out[l].astype(BF16),
            w_e_gate=w_e_gate[l].astype(BF16), w_e_up=w_e_up[l].astype(BF16), w_e_down=w_e_down[l].astype(BF16),
        )
        m = mods[l]
        mod_lat = [m[:bsz, i * d:(i + 1) * d][:, None, :] for i in range(6)]
        mod_ctx = [jnp.broadcast_to(m[bsz:bsz + 1, i * d:(i + 1) * d][:, None, :], (bsz, 1, d)) for i in range(6)]

        puv_c, qkt_c = _project(x_ctx, lw, mod_ctx, tl_ctx)
        puv, qkt = _project(x_lat, lw, mod_lat, tl_lat)
        seg_norm = lambda g: jnp.sqrt(QK_NOPE * jnp.max(g[:QK_NOPE] ** 2) + QK_ROPE * jnp.max(g[QK_NOPE:] ** 2))
        bound = QK_EXP2_SCALE * seg_norm(q_norm_g[l]) * seg_norm(k_norm_g[l])
        qpad = jnp.zeros((HEAD_PAD - QK_HEAD, KEY_TILE), F32).at[0].set(-bound)
        qt, k, vt = _qkv(qkt, qkt_c, bro(qa_norm_g[l]), bro(kva_norm_g[l]), bro(q_norm_g[l]), bro(k_norm_g[l]),
                         cos_t, sin_t, qpad, lw["w_uq_t"], lw["w_ukv_t"])
        lat_attn = functools.partial(_attention, qt, k, vt, 0, n_lat // tl_lat["attn"], tl_lat["attn"], 0, nl + nc)
        yat = lax.cond(2.0 * bound < EXP2_SAFE_RANGE, lambda: lat_attn(False), lambda: lat_attn(True))
        streams = [_mixer(x_lat, puv, yat, lw, mod_lat, rw, tl_lat) + (mod_lat[5], n_lat)]
        if not last:
            yat_c = _attention(qt, k, vt, nl, nc, KEY_TILE, nl, nc, True)
            streams.append(_mixer(x_ctx, puv_c, yat_c, lw, mod_ctx, rw, tl_ctx) + (mod_ctx[5], n_ctx))
        outs = _moe_block(streams, lw, rw, router_b)
        x_lat = outs[0].reshape(bsz, n_lat, d)
        if not last:
            x_ctx = outs[1].reshape(bsz, n_ctx, d)
    return x_lat
```
